```python
import math
import jax, jax.numpy as jnp
from jax import lax
import numpy as np

D_MODEL = 4096
BATCH = 4
SEQ = 2048
DEPTH = 4
DEC_BATCH = 8
DEC_SEQ = 8
PAST_LEN = 8192
PAGE_SIZE = 128

N_META = 16
D_INNER = D_MODEL
SSD_HEAD_DIM = 64
N_SSD_HEADS = D_INNER // SSD_HEAD_DIM
N_SSD_GROUPS = 8
SSD_STATE = 128
CONV_WIDTH = 4
CONV_DIM = D_INNER + 2 * N_SSD_GROUPS * SSD_STATE
SSD_CHUNK = 128
SB_HEAD_DIM = 128
N_SB_HEADS = D_MODEL // (2 * SB_HEAD_DIM)
SB_WIDTH = N_SB_HEADS * SB_HEAD_DIM
SB_BLOCK = 128
SB_SCALE = SB_HEAD_DIM ** -0.5
SB_BIAS_INIT = -6.0
N_GROUPS = 4
EXPERTS_PER_GROUP = 4
N_EXPERTS = N_GROUPS * EXPERTS_PER_GROUP
TOP_K_IN_GROUP = 2
D_FF_EXPERT = D_MODEL // 4
DEEPNORM_ALPHA = (2 * DEPTH) ** 0.25
DEEPNORM_BETA = (8 * DEPTH) ** -0.25
LN_EPS = 1e-5
RMS_EPS = 1e-5
IN_SIZES = (D_INNER, CONV_DIM, N_SSD_HEADS, SB_WIDTH, SB_WIDTH, SB_WIDTH, D_MODEL, D_MODEL)
IN_COLS = sum(IN_SIZES)

kernel_name = 'hybrid_ssd_stickbreak_hmoe_step'


def layernorm(x, g, b):
    xf = x.astype(jnp.float32)
    mu = jnp.mean(xf, axis=-1, keepdims=True)
    var = jnp.mean(jnp.square(xf - mu), axis=-1, keepdims=True)
    return ((xf - mu) * lax.rsqrt(var + LN_EPS) * g + b).astype(x.dtype)


def split_projection(h, w_in):
    u = jnp.einsum('btd,dc->btc', h, w_in)
    idx = [int(i) for i in np.cumsum(IN_SIZES)[:-1]]
    return jnp.split(u, idx, axis=-1)


def causal_conv(xbc, conv_prev, w, b):
    t = xbc.shape[1]
    xpad = jnp.concatenate([conv_prev.astype(xbc.dtype), xbc], axis=1)
    out = sum(xpad[:, i:i + t] * w[i] for i in range(CONV_WIDTH)) + b
    return jax.nn.silu(out), xpad[:, t:]


def ssd_chunked(x, dt, a, bmat, cmat, h0, chunk):
    bsz, length = x.shape[:2]
    nc = length // chunk
    hg = N_SSD_HEADS // N_SSD_GROUPS
    x = x.reshape(bsz, nc, chunk, N_SSD_GROUPS, hg, SSD_HEAD_DIM)
    dt = dt.reshape(bsz, nc, chunk, N_SSD_GROUPS, hg)
    bmat = bmat.reshape(bsz, nc, chunk, N_SSD_GROUPS, SSD_STATE)
    cmat = cmat.reshape(bsz, nc, chunk, N_SSD_GROUPS, SSD_STATE)
    acum = jnp.cumsum(dt * a.reshape(N_SSD_GROUPS, hg), axis=2)
    tri = jnp.tril(jnp.ones((chunk, chunk), dtype=bool))
    seg = acum[:, :, :, None] - acum[:, :, None, :]
    decay = jnp.exp(jnp.where(tri[:, :, None, None], seg, -jnp.inf))
    cb = jnp.einsum('bcqgn,bcsgn->bcqsg', cmat, bmat)
    m = cb[..., None] * decay * dt[:, :, None]
    y_diag = jnp.einsum('bcqsgh,bcsghp->bcqghp', m, x)
    to_end = jnp.exp(acum[:, :, -1:] - acum) * dt
    states = jnp.einsum('bcsgn,bcsgh,bcsghp->bcghpn', bmat, to_end, x)
    chunk_decay = jnp.exp(acum[:, :, -1])
    h0 = h0.reshape(bsz, N_SSD_GROUPS, hg, SSD_HEAD_DIM, SSD_STATE)

    def step(hc, inp):
        st, dec = inp
        return hc * dec[..., None, None] + st, hc

    h_last, h_prev = lax.scan(step, h0, (jnp.moveaxis(states, 1, 0), jnp.moveaxis(chunk_decay, 1, 0)))
    h_prev = jnp.moveaxis(h_prev, 0, 1)
    y_off = jnp.einsum('bcqgn,bcqgh,bcghpn->bcqghp', cmat, jnp.exp(acum), h_prev)
    y = (y_diag + y_off).reshape(bsz, length, N_SSD_HEADS, SSD_HEAD_DIM)
    return y, h_last.reshape(bsz, N_SSD_HEADS, SSD_HEAD_DIM, SSD_STATE)


def ssd_branch(xbc_act, z, dt_raw, h0, segments, dt_bias, a_log, d_skip, norm_g):
    bsz, length = z.shape[:2]
    xs, bm, cm = jnp.split(xbc_act.astype(jnp.float32), [D_INNER, D_INNER + N_SSD_GROUPS * SSD_STATE], axis=-1)
    xs = xs.reshape(bsz, length, N_SSD_HEADS, SSD_HEAD_DIM)
    bm = bm.reshape(bsz, length, N_SSD_GROUPS, SSD_STATE)
    cm = cm.reshape(bsz, length, N_SSD_GROUPS, SSD_STATE)
    dt = jax.nn.softplus(dt_raw.astype(jnp.float32) + dt_bias)
    a = -jnp.exp(a_log.astype(jnp.float32))
    hstate = h0.astype(jnp.float32)
    ys = []
    start = 0
    for seg_len, chunk in segments:
        sl = slice(start, start + seg_len)
        y_seg, hstate = ssd_chunked(xs[:, sl], dt[:, sl], a, bm[:, sl], cm[:, sl], hstate, chunk)
        ys.append(y_seg)
        start += seg_len
    y = jnp.concatenate(ys, axis=1) + d_skip[:, None] * xs
    y = y.reshape(bsz, length, D_INNER) * jax.nn.silu(z.astype(jnp.float32))
    yg = y.reshape(bsz, length, N_SSD_GROUPS, D_INNER // N_SSD_GROUPS)
    yg = yg * lax.rsqrt(jnp.mean(yg * yg, axis=-1, keepdims=True) + RMS_EPS)
    y = yg.reshape(bsz, length, D_INNER) * norm_g
    return y.astype(z.dtype), hstate


def sb_attend(q, k, v, q_pos, k_pos, sb_bias):
    z = jnp.einsum('bqhd,bkhd->bhqk', q, k).astype(jnp.float32) * SB_SCALE
    z = z + sb_bias.astype(jnp.float32)[None, :, None, None]
    causal = k_pos[None, :] < q_pos[:, None]
    lneg = jnp.where(causal, jax.nn.log_sigmoid(-z), 0.0)
    between = lax.cumsum(lneg, axis=3, reverse=True) - lneg
    w = jnp.where(causal, jnp.exp(jax.nn.log_sigmoid(z) + between), 0.0)
    return jnp.einsum('bhqk,bkhd->bqhd', w.astype(v.dtype), v)


def sb_prompt(q, k, v, sb_bias):
    bsz, length = q.shape[:2]
    pos = jnp.arange(length, dtype=jnp.int32)
    o_meta = sb_attend(q[:, :N_META], k[:, :N_META], v[:, :N_META], pos[:N_META], pos[:N_META], sb_bias)
    n_blocks = (length - N_META) // SB_BLOCK

    def block(i):
        start = N_META + i * SB_BLOCK
        qb = lax.dynamic_slice_in_dim(q, start, SB_BLOCK, axis=1)
        qp = start + jnp.arange(SB_BLOCK, dtype=jnp.int32)
        return sb_attend(qb, k, v, qp, pos, sb_bias)

    o_real = lax.map(block, jnp.arange(n_blocks, dtype=jnp.int32))
    o_real = jnp.moveaxis(o_real, 0, 1).reshape(bsz, length - N_META, N_SB_HEADS, SB_HEAD_DIM)
    return jnp.concatenate([o_meta, o_real], axis=1)


def sb_sample(q, k, v, k_past, v_past, sb_bias):
    past_len = k_past.shape[1]
    k_all = jnp.concatenate([k_past.astype(k.dtype), k], axis=1)
    v_all = jnp.concatenate([v_past.astype(v.dtype), v], axis=1)
    q_pos = past_len + jnp.arange(q.shape[1], dtype=jnp.int32)
    k_pos = jnp.arange(k_all.shape[1], dtype=jnp.int32)
    return sb_attend(q, k_all, v_all, q_pos, k_pos, sb_bias)


def gather_pages(cache, page_table, l):
    rows = cache[page_table, l]
    return rows.reshape(rows.shape[0], -1, N_SB_HEADS, SB_HEAD_DIM)


def hier_moe(h, wr_group, br_group, wr_expert, br_expert, w_gate, w_up, w_down):
    grp_prob = jax.nn.softmax((jnp.einsum('btd,dg->btg', h, wr_group) + br_group).astype(jnp.float32), axis=-1)
    g_val, g_idx = lax.top_k(grp_prob, 1)
    exp_logits = (jnp.einsum('btd,gde->btge', h, wr_expert) + br_expert).astype(jnp.float32)
    sel_logits = jnp.einsum('btg,btge->bte', jax.nn.one_hot(g_idx[..., 0], N_GROUPS, dtype=jnp.float32), exp_logits)
    e_val, e_idx = lax.top_k(sel_logits, TOP_K_IN_GROUP)
    e_w = jax.nn.softmax(e_val, axis=-1) * g_val
    expert_id = g_idx * EXPERTS_PER_GROUP + e_idx
    gates = jnp.sum(jax.nn.one_hot(expert_id, N_EXPERTS, dtype=jnp.float32) * e_w[..., None], axis=2)
    hid = jax.nn.silu(jnp.einsum('btd,edf->btef', h, w_gate)) * jnp.einsum('btd,edf->btef', h, w_up)
    hid = hid * gates[..., None].astype(hid.dtype)
    return jnp.einsum('btef,efd->btd', hid, w_down)


def trunk_layer(h, lw, conv_prev, ssm_prev, k_past, v_past, segments):
    (w_in, conv_w, conv_b, dt_bias, a_log, d_skip, ssd_norm_g, w_ssd_out, w_sb_out, sb_bias, w_o,
     ln1_g, ln1_b, wr_group, br_group, wr_expert, br_expert, w_gate, w_up, w_down, ln2_g, ln2_b) = lw
    bsz, t = h.shape[:2]
    z, xbc, dt_raw, q, k, v, g_ssd, g_sb = split_projection(h, w_in)
    xbc_act, conv_new = causal_conv(xbc, conv_prev, conv_w, conv_b)
    y_ssd, ssm_new = ssd_branch(xbc_act, z, dt_raw, ssm_prev, segments, dt_bias, a_log, d_skip, ssd_norm_g)
    q = q.reshape(bsz, t, N_SB_HEADS, SB_HEAD_DIM)
    k = k.reshape(bsz, t, N_SB_HEADS, SB_HEAD_DIM)
    v = v.reshape(bsz, t, N_SB_HEADS, SB_HEAD_DIM)
    if k_past is None:
        o_sb = sb_prompt(q, k, v, sb_bias)
    else:
        o_sb = sb_sample(q, k, v, k_past, v_past, sb_bias)
    merged = (jax.nn.sigmoid(g_ssd) * jnp.einsum('bti,id->btd', y_ssd, w_ssd_out)
              + jax.nn.sigmoid(g_sb) * jnp.einsum('bti,id->btd', o_sb.reshape(bsz, t, SB_WIDTH), w_sb_out))
    mix = jnp.einsum('btd,de->bte', merged, w_o)
    h = layernorm(DEEPNORM_ALPHA * h + mix, ln1_g, ln1_b)
    h = layernorm(DEEPNORM_ALPHA * h + hier_moe(h, wr_group, br_group, wr_expert, br_expert, w_gate, w_up, w_down), ln2_g, ln2_b)
    return h, conv_new, ssm_new.astype(h.dtype), k, v


def setup_inputs(seed: int = 0) -> dict:
    key = jax.random.key(seed)
    ks = jax.random.split(key, 32)
    n_pages = PAST_LEN // PAGE_SIZE
    n_pool = (DEC_BATCH * n_pages * 5) // 4

    def nrm(k, shape, scale):
        return jax.random.normal(k, shape, jnp.float32) * scale

    x_prompt = nrm(ks[0], (BATCH, SEQ, D_MODEL), 1.0)
    x_sample = nrm(ks[1], (DEC_BATCH, DEC_SEQ, D_MODEL), 1.0)
    cache_k = nrm(ks[2], (n_pool, DEPTH, PAGE_SIZE, N_SB_HEADS, SB_HEAD_DIM), 1.0)
    cache_v = nrm(ks[3], (n_pool, DEPTH, PAGE_SIZE, N_SB_HEADS, SB_HEAD_DIM), 1.0)
    page_table = jax.random.permutation(ks[4], n_pool)[:DEC_BATCH * n_pages].reshape(DEC_BATCH, n_pages).astype(jnp.int32)
    state_ssm = nrm(ks[5], (DEPTH, DEC_BATCH, N_SSD_HEADS, SSD_HEAD_DIM, SSD_STATE), 0.1)
    state_conv = nrm(ks[6], (DEPTH, DEC_BATCH, CONV_WIDTH - 1, CONV_DIM), 1.0)
    meta_tokens = nrm(ks[7], (N_META, D_MODEL), 1.0)
    ln_emb_g = 1.0 + nrm(ks[8], (D_MODEL,), 0.02)
    ln_emb_b = nrm(ks[9], (D_MODEL,), 0.02)
    w_in = nrm(ks[10], (DEPTH, D_MODEL, IN_COLS), D_MODEL ** -0.5)
    conv_w = nrm(ks[11], (DEPTH, CONV_WIDTH, CONV_DIM), CONV_WIDTH ** -0.5)
    conv_b = nrm(ks[12], (DEPTH, CONV_DIM), 0.02)
    dt0 = jnp.exp(jax.random.uniform(ks[13], (DEPTH, N_SSD_HEADS), jnp.float32, math.log(1e-3), math.log(1e-1)))
    dt_bias = dt0 + jnp.log(-jnp.expm1(-dt0))
    a_log = jnp.log(jax.random.uniform(ks[14], (DEPTH, N_SSD_HEADS), jnp.float32, 1.0, 16.0))
    d_skip = 1.0 + nrm(ks[15], (DEPTH, N_SSD_HEADS), 0.02)
    ssd_norm_g = 1.0 + nrm(ks[16], (DEPTH, D_INNER), 0.02)
    w_ssd_out = nrm(ks[17], (DEPTH, D_INNER, D_MODEL), D_INNER ** -0.5 * DEEPNORM_BETA)
    w_sb_out = nrm(ks[18], (DEPTH, SB_WIDTH, D_MODEL), SB_WIDTH ** -0.5 * DEEPNORM_BETA)
    sb_bias = SB_BIAS_INIT + nrm(ks[31], (DEPTH, N_SB_HEADS), 0.1)
    w_o = nrm(ks[19], (DEPTH, D_MODEL, D_MODEL), D_MODEL ** -0.5 * DEEPNORM_BETA)
    ln1_g = 1.0 + nrm(ks[20], (DEPTH, D_MODEL), 0.02)
    ln1_b = nrm(ks[21], (DEPTH, D_MODEL), 0.02)
    wr_group = nrm(ks[22], (DEPTH, D_MODEL, N_GROUPS), D_MODEL ** -0.5)
    br_group = nrm(ks[23], (DEPTH, N_GROUPS), 0.01)
    wr_expert = nrm(ks[24], (DEPTH, N_GROUPS, D_MODEL, EXPERTS_PER_GROUP), D_MODEL ** -0.5)
    br_expert = nrm(ks[25], (DEPTH, N_GROUPS, EXPERTS_PER_GROUP), 0.01)
    w_gate = nrm(ks[26], (DEPTH, N_EXPERTS, D_MODEL, D_FF_EXPERT), D_MODEL ** -0.5)
    w_up = nrm(ks[27], (DEPTH, N_EXPERTS, D_MODEL, D_FF_EXPERT), D_MODEL ** -0.5)
    w_down = nrm(ks[28], (DEPTH, N_EXPERTS, D_FF_EXPERT, D_MODEL), D_FF_EXPERT ** -0.5 * DEEPNORM_BETA)
    ln2_g = 1.0 + nrm(ks[29], (DEPTH, D_MODEL), 0.02)
    ln2_b = nrm(ks[30], (DEPTH, D_MODEL), 0.02)
    return {'x_prompt': x_prompt, 'x_sample': x_sample, 'cache_k': cache_k, 'cache_v': cache_v,
            'page_table': page_table, 'state_ssm': state_ssm, 'state_conv': state_conv,
            'meta_tokens': meta_tokens, 'ln_emb_g': ln_emb_g, 'ln_emb_b': ln_emb_b,
            'w_in': w_in, 'conv_w': conv_w, 'conv_b': conv_b, 'dt_bias': dt_bias, 'a_log': a_log,
            'd_skip': d_skip, 'ssd_norm_g': ssd_norm_g, 'w_ssd_out': w_ssd_out, 'w_sb_out': w_sb_out,
            'sb_bias': sb_bias, 'w_o': w_o, 'ln1_g': ln1_g, 'ln1_b': ln1_b, 'wr_group': wr_group,
            'br_group': br_group, 'wr_expert': wr_expert, 'br_expert': br_expert, 'w_gate': w_gate,
            'w_up': w_up, 'w_down': w_down, 'ln2_g': ln2_g, 'ln2_b': ln2_b}


def reference(x_prompt, x_sample, cache_k, cache_v, page_table, state_ssm, state_conv, meta_tokens,
              ln_emb_g, ln_emb_b, w_in, conv_w, conv_b, dt_bias, a_log, d_skip, ssd_norm_g,
              w_ssd_out, w_sb_out, sb_bias, w_o, ln1_g, ln1_b, wr_group, br_group, wr_expert, br_expert,
              w_gate, w_up, w_down, ln2_g, ln2_b):
    bp, seq = x_prompt.shape[:2]
    dseq = x_sample.shape[1]
    meta = jnp.broadcast_to(meta_tokens[None].astype(x_prompt.dtype), (bp, N_META, D_MODEL))
    hp = layernorm(jnp.concatenate([meta, x_prompt], axis=1), ln_emb_g, ln_emb_b)
    hs = layernorm(x_sample, ln_emb_g, ln_emb_b)
    seg_prompt = ((N_META, N_META), (seq, SSD_CHUNK))
    seg_sample = ((dseq, dseq),)
    conv0 = jnp.zeros((bp, CONV_WIDTH - 1, CONV_DIM), hp.dtype)
    ssm0 = jnp.zeros((bp, N_SSD_HEADS, SSD_HEAD_DIM, SSD_STATE), jnp.float32)
    kp, vp, sp, cp = [], [], [], []
    ksm, vsm, ssm, csm = [], [], [], []
    for l in range(DEPTH):
        lw = (w_in[l], conv_w[l], conv_b[l], dt_bias[l], a_log[l], d_skip[l], ssd_norm_g[l],
              w_ssd_out[l], w_sb_out[l], sb_bias[l], w_o[l], ln1_g[l], ln1_b[l], wr_group[l], br_group[l],
              wr_expert[l], br_expert[l], w_gate[l], w_up[l], w_down[l], ln2_g[l], ln2_b[l])
        hp, c_new, s_new, k_new, v_new = trunk_layer(hp, lw, conv0, ssm0, None, None, seg_prompt)
        kp.append(k_new); vp.append(v_new); sp.append(s_new); cp.append(c_new)
        k_past = gather_pages(cache_k, page_table, l)
        v_past = gather_pages(cache_v, page_table, l)
        hs, c_new, s_new, k_new, v_new = trunk_layer(hs, lw, state_conv[l], state_ssm[l], k_past, v_past, seg_sample)
        ksm.append(k_new); vsm.append(v_new); ssm.append(s_new); csm.append(c_new)
    return (hp[:, N_META:], hs, jnp.stack(kp), jnp.stack(vp), jnp.stack(sp), jnp.stack(cp),
            jnp.stack(ksm), jnp.stack(vsm), jnp.stack(ssm), jnp.stack(csm))
```

```python
import functools
import math

import jax
import jax.numpy as jnp
from jax import lax
from jax.experimental import pallas as pl
from jax.experimental.pallas import tpu as pltpu

F32 = jnp.float32
BF16 = jnp.bfloat16

N_META = 16
SSD_CHUNK = 128
N_SSD_GROUPS = 8
N_ROUTE_GROUPS = 4
LN_EPS = 1e-5
RMS_EPS = 1e-5
NEG_BIG = -1e30

LANES = 128
SUBLANES = 8
VMEM_LIMIT_BYTES = 56 * 1024 * 1024


def _cparams(sem):
    return pltpu.CompilerParams(dimension_semantics=sem, vmem_limit_bytes=VMEM_LIMIT_BYTES)


def _largest_tile(n, cap, mult):
    best = None
    for t in range(mult, min(n, cap) + 1, mult):
        if n % t == 0:
            best = t
    assert best is not None, (n, cap, mult)
    return best


def _dot(a, b):
    return jnp.dot(a, b, preferred_element_type=F32)


def _dot_nt(a, b):
    return lax.dot_general(a, b, (((1,), (1,)), ((), ())), preferred_element_type=F32)


def _dot_tn(a, b):
    return lax.dot_general(a, b, (((0,), (0,)), ((), ())), preferred_element_type=F32)


def _split3(x):
    hi = x.astype(BF16)
    r = x - hi.astype(F32)
    mid = r.astype(BF16)
    lo = (r - mid.astype(F32)).astype(BF16)
    return hi, mid, lo


def _split2(x):
    hi = x.astype(BF16)
    lo = (x - hi.astype(F32)).astype(BF16)
    return hi, lo


def _softplus(x):
    return jnp.maximum(x, 0.0) + jnp.log(1.0 + jnp.exp(-jnp.abs(x)))


def _sigmoid(x):
    return 1.0 / (1.0 + jnp.exp(-x))


def _layernorm(x, g, b):
    mu = jnp.mean(x, axis=-1, keepdims=True)
    xc = x - mu
    var = jnp.mean(xc * xc, axis=-1, keepdims=True)
    return xc * lax.rsqrt(var + LN_EPS) * g + b


def _ln_kernel(x_ref, g_ref, b_ref, of_ref, ob_ref):
    y = _layernorm(x_ref[...], g_ref[...], b_ref[...])
    of_ref[...] = y
    ob_ref[...] = y.astype(BF16)


def _embed_ln(x, g, b):
    n, d = x.shape
    tm = _largest_tile(n, 320, 16)
    return pl.pallas_call(
        _ln_kernel,
        grid=(n // tm,),
        in_specs=[pl.BlockSpec((tm, d), lambda i: (i, 0)),
                  pl.BlockSpec((1, d), lambda i: (0, 0)),
                  pl.BlockSpec((1, d), lambda i: (0, 0))],
        out_specs=[pl.BlockSpec((tm, d), lambda i: (i, 0)),
                   pl.BlockSpec((tm, d), lambda i: (i, 0))],
        out_shape=[jax.ShapeDtypeStruct((n, d), F32), jax.ShapeDtypeStruct((n, d), BF16)],
        compiler_params=_cparams(("parallel",)),
        name="embed_ln",
    )(x, g.reshape(1, d), b.reshape(1, d))


def _mm_kernel(x_ref, w_ref, o_ref):
    o_ref[...] = _dot(x_ref[...], w_ref[...]).astype(o_ref.dtype)


def _matmul(x, w, out_dtype, name):
    n, k = x.shape
    m = w.shape[1]
    tm = _largest_tile(n, 640, 16)
    tn = _largest_tile(m, 1024, LANES)
    return pl.pallas_call(
        _mm_kernel,
        grid=(n // tm, m // tn),
        in_specs=[pl.BlockSpec((tm, k), lambda i, j: (i, 0)),
                  pl.BlockSpec((k, tn), lambda i, j: (0, j))],
        out_specs=pl.BlockSpec((tm, tn), lambda i, j: (i, j)),
        out_shape=jax.ShapeDtypeStruct((n, m), out_dtype),
        compiler_params=_cparams(("parallel", "parallel")),
        name=name,
    )(x, w)


def _ssd_chunk(xpad_ref, act_ref, st_ref, z, dtraw, n_valid, cw_ref, cb_ref, dtb_ref, aneg_ref, dskip_ref,
               ng_ref, eexp_ref, y_write, d_model, n_heads):
    q = SSD_CHUNK
    c_dim = xpad_ref.shape[1]
    nst = st_ref.shape[0]
    g_cnt = N_SSD_GROUPS
    gw = d_model // g_cnt
    heads_per_group = n_heads // g_cnt
    b_off = d_model
    c_off = d_model + g_cnt * nst

    ct = 512
    for c0 in range(0, c_dim, ct):
        acc = cb_ref[:, c0:c0 + ct]
        for i in range(4):
            acc = acc + xpad_ref[pl.ds(SUBLANES - 3 + i, q), c0:c0 + ct] * cw_ref[i:i + 1, c0:c0 + ct]
        act_ref[:, c0:c0 + ct] = acc * _sigmoid(acc)

    rows = lax.broadcasted_iota(jnp.int32, (q, q), 0)
    cols = lax.broadcasted_iota(jnp.int32, (q, q), 1)
    tri = cols <= rows
    tril_ones = jnp.where(tri, 1.0, 0.0).astype(BF16)
    low_half = cols < (LANES // 2)

    dt = _softplus(dtraw + dtb_ref[...])
    if n_valid < q:
        dt = jnp.where(rows < n_valid, dt, 0.0)
    da = dt * aneg_ref[...]
    d_hi, d_mid, d_lo = _split3(da)
    acum = _dot(tril_ones, d_hi) + _dot(tril_ones, d_mid) + _dot(tril_ones, d_lo)
    last = acum[q - 1:q, :]
    w_end = jnp.exp(last - acum) * dt
    acum_t = acum.T
    dt_t = dt.T
    eexp = eexp_ref[...]
    w_exp = _dot(w_end.astype(BF16), eexp)
    cd = jnp.broadcast_to(jnp.exp(last), (SUBLANES, LANES))
    cd_hi, cd_lo = _split2(cd)
    cd_exp = (_dot(cd_hi, eexp) + _dot(cd_lo, eexp))[0:1, :]

    for g in range(g_cnt):
        bm = act_ref[:, b_off + g * nst:b_off + (g + 1) * nst]
        cm = act_ref[:, c_off + g * nst:c_off + (g + 1) * nst]
        bm_b = bm.astype(BF16)
        cb_mat = _dot_nt(cm.astype(BF16), bm_b)
        ys = []
        for pr in range(heads_per_group // 2):
            col0 = g * gw + pr * LANES
            lhs = []
            for hh in range(2):
                h = g * heads_per_group + 2 * pr + hh
                colb = jnp.broadcast_to(acum[:, h:h + 1], (q, q))
                seg = colb - acum_t[h:h + 1, :]
                dec = jnp.exp(jnp.where(tri, seg, NEG_BIG))
                lhs.append((cb_mat * dec * dt_t[h:h + 1, :]).astype(BF16))
                lhs.append((cm * jnp.exp(colb)).astype(BF16))
            xp = act_ref[:, col0:col0 + LANES]
            sp = st_ref[:, col0:col0 + LANES]
            rhs = jnp.concatenate([jnp.where(low_half, xp, 0.0), jnp.where(low_half, sp, 0.0),
                                   jnp.where(low_half, 0.0, xp), jnp.where(low_half, 0.0, sp)], axis=0).astype(BF16)
            yp = _dot(jnp.concatenate(lhs, axis=1), rhs)
            yp = yp + dskip_ref[:, col0:col0 + LANES] * xp
            zp = z[:, col0:col0 + LANES]
            ys.append(yp * (zp * _sigmoid(zp)))
        ss = ys[0] * ys[0]
        for yp in ys[1:]:
            ss = ss + yp * yp
        inv = lax.rsqrt(jnp.sum(ss, axis=-1, keepdims=True) / gw + RMS_EPS)
        for pr, yp in enumerate(ys):
            col0 = g * gw + pr * LANES
            y_write(col0, yp * inv * ng_ref[:, col0:col0 + LANES])
        xg = act_ref[:, g * gw:(g + 1) * gw]
        xw = (xg * w_exp[:, g * gw:(g + 1) * gw]).astype(BF16)
        dst = _dot_tn(bm_b, xw)
        st_ref[:, g * gw:(g + 1) * gw] = st_ref[:, g * gw:(g + 1) * gw] * cd_exp[:, g * gw:(g + 1) * gw] + dst


def _ssd_main_kernel(xbc_ref, xmeta_ref, z_ref, dt_ref, st0_ref, cw_ref, cb_ref, dtb_ref, aneg_ref, dskip_ref,
                     ng_ref, eexp_ref, y_ref, stout_ref, xpad_ref, act_ref, st_ref, *, d_model, n_heads):
    c = pl.program_id(1)

    @pl.when(c == 0)
    def _():
        st_ref[...] = st0_ref[0]
        xpad_ref[0:SUBLANES, :] = xmeta_ref[N_META - SUBLANES:N_META, :]

    xpad_ref[SUBLANES:SUBLANES + SSD_CHUNK, :] = xbc_ref[...]

    def y_write(col0, val):
        y_ref[:, col0:col0 + val.shape[1]] = val.astype(y_ref.dtype)

    _ssd_chunk(xpad_ref, act_ref, st_ref, z_ref[...].astype(F32), dt_ref[...], SSD_CHUNK, cw_ref, cb_ref, dtb_ref,
               aneg_ref, dskip_ref, ng_ref, eexp_ref, y_write, d_model, n_heads)
    xpad_ref[0:SUBLANES, :] = xbc_ref[SSD_CHUNK - SUBLANES:SSD_CHUNK, :]

    @pl.when(c == pl.num_programs(1) - 1)
    def _():
        stout_ref[0] = st_ref[...]


def _ssd_small_kernel(x_ref, prev_ref, z_ref, dt_ref, st0_ref, cw_ref, cb_ref, dtb_ref, aneg_ref, dskip_ref,
                      ng_ref, eexp_ref, y_ref, stout_ref, xpad_ref, act_ref, st_ref, *, d_model, n_heads, seq):
    st_ref[...] = st0_ref[0]
    xpad_ref[0:SUBLANES, :] = prev_ref[0]
    xpad_ref[SUBLANES:SUBLANES + seq, :] = x_ref[...]
    xpad_ref[SUBLANES + seq:, :] = jnp.zeros((SSD_CHUNK - seq, xpad_ref.shape[1]), F32)
    pad = jnp.zeros((SSD_CHUNK - seq, d_model), F32)
    z = jnp.concatenate([z_ref[...], pad], axis=0)
    dtr = jnp.concatenate([dt_ref[...], jnp.zeros((SSD_CHUNK - seq, LANES), F32)], axis=0)

    def y_write(col0, val):
        y_ref[:, col0:col0 + val.shape[1]] = val[0:seq, :]

    _ssd_chunk(xpad_ref, act_ref, st_ref, z, dtr, seq, cw_ref, cb_ref, dtb_ref, aneg_ref, dskip_ref, ng_ref,
               eexp_ref, y_write, d_model, n_heads)
    stout_ref[0] = st_ref[...]


def _ssd_consts_specs(c_dim, d_model, idx):
    return [pl.BlockSpec((SUBLANES, c_dim), idx), pl.BlockSpec((1, c_dim), idx), pl.BlockSpec((1, LANES), idx),
            pl.BlockSpec((1, LANES), idx), pl.BlockSpec((1, d_model), idx), pl.BlockSpec((1, d_model), idx),
            pl.BlockSpec((LANES, d_model), idx)]


def _ssd_main(xbc, z_src, dt, st0, consts, n_batch, seq, d_model, n_heads, nst):
    c_dim = xbc.shape[1]
    nc = seq // SSD_CHUNK
    meta_blk0 = n_batch * seq // N_META
    kern = functools.partial(_ssd_main_kernel, d_model=d_model, n_heads=n_heads)
    return pl.pallas_call(
        kern,
        grid=(n_batch, nc),
        in_specs=[pl.BlockSpec((SSD_CHUNK, c_dim), lambda b, c: (b * nc + c, 0)),
                  pl.BlockSpec((N_META, c_dim), lambda b, c: (meta_blk0 + b, 0)),
                  pl.BlockSpec((SSD_CHUNK, d_model), lambda b, c: (b * nc + c, 0)),
                  pl.BlockSpec((SSD_CHUNK, LANES), lambda b, c: (b * nc + c, 0)),
                  pl.BlockSpec((1, nst, d_model), lambda b, c: (b, 0, 0))]
        + _ssd_consts_specs(c_dim, d_model, lambda b, c: (0, 0)),
        out_specs=[pl.BlockSpec((SSD_CHUNK, d_model), lambda b, c: (b * nc + c, 0)),
                   pl.BlockSpec((1, nst, d_model), lambda b, c: (b, 0, 0))],
        out_shape=[jax.ShapeDtypeStruct((n_batch * seq, d_model), BF16),
                   jax.ShapeDtypeStruct((n_batch, nst, d_model), F32)],
        scratch_shapes=[pltpu.VMEM((SUBLANES + SSD_CHUNK, c_dim), F32), pltpu.VMEM((SSD_CHUNK, c_dim), F32),
                        pltpu.VMEM((nst, d_model), F32)],
        compiler_params=_cparams(("parallel", "arbitrary")),
        name="ssd_main",
    )(xbc, xbc, z_src, dt, st0, *consts)


def _ssd_small(x, prev, z, dt, st0, consts, n_seq, seq, d_model, n_heads, nst, name):
    c_dim = x.shape[1]
    kern = functools.partial(_ssd_small_kernel, d_model=d_model, n_heads=n_heads, seq=seq)
    return pl.pallas_call(
        kern,
        grid=(n_seq,),
        in_specs=[pl.BlockSpec((seq, c_dim), lambda i: (i, 0)),
                  pl.BlockSpec((1, SUBLANES, c_dim), lambda i: (i, 0, 0)),
                  pl.BlockSpec((seq, d_model), lambda i: (i, 0)),
                  pl.BlockSpec((seq, LANES), lambda i: (i, 0)),
                  pl.BlockSpec((1, nst, d_model), lambda i: (i, 0, 0))]
        + _ssd_consts_specs(c_dim, d_model, lambda i: (0, 0)),
        out_specs=[pl.BlockSpec((seq, d_model), lambda i: (i, 0)),
                   pl.BlockSpec((1, nst, d_model), lambda i: (i, 0, 0))],
        out_shape=[jax.ShapeDtypeStruct((n_seq * seq, d_model), F32),
                   jax.ShapeDtypeStruct((n_seq, nst, d_model), F32)],
        scratch_shapes=[pltpu.VMEM((SUBLANES + SSD_CHUNK, c_dim), F32), pltpu.VMEM((SSD_CHUNK, c_dim), F32),
                        pltpu.VMEM((nst, d_model), F32)],
        compiler_params=_cparams(("parallel",)),
        name=name,
    )(x, prev, z, dt, st0, *consts)


def _sb_tile(s_t, bias, scale, visible, run):
    nk = s_t.shape[0]
    sub = lax.broadcasted_iota(jnp.int32, (SUBLANES, s_t.shape[1]), 0)
    w_groups = [None] * (nk // SUBLANES)
    for gi in reversed(range(nk // SUBLANES)):
        r0 = gi * SUBLANES
        z = s_t[r0:r0 + SUBLANES, :] * scale + bias
        zc = jnp.minimum(z, 30.0)
        sp = jnp.where(z > 30.0, z, jnp.log(1.0 + jnp.exp(zc)))
        if visible is not None:
            vis = visible(sub + r0)
            lneg = jnp.where(vis, -sp, 0.0)
        else:
            lneg = -sp
        y = lneg + jnp.where(sub < 7, pltpu.roll(lneg, 7, 0), 0.0)
        y = y + jnp.where(sub < 6, pltpu.roll(y, 6, 0), 0.0)
        y = y + jnp.where(sub < 4, pltpu.roll(y, 4, 0), 0.0)
        between = y - lneg + run
        w = jnp.exp(z - sp + between)
        if visible is not None:
            w = jnp.where(vis, w, 0.0)
        w_groups[gi] = w
        run = run + y[0:1, :]
    return jnp.concatenate(w_groups, axis=0), run


def _sb_prompt_kernel(bias_ref, q_ref, qm_ref, k_ref, km_ref, v_ref, vm_ref, o_ref, om_ref, kb_ref, vt_ref,
                      *, seq, tq, scale):
    h = pl.program_id(1)
    bias = bias_ref[h]
    n_t = seq // tq
    hd = k_ref.shape[1]
    for t in range(n_t):
        kb_ref[t * tq:(t + 1) * tq, :] = k_ref[t * tq:(t + 1) * tq, :].astype(BF16)
        vt_ref[:, t * tq:(t + 1) * tq] = v_ref[t * tq:(t + 1) * tq, :].T.astype(BF16)
    zpad = jnp.zeros((LANES - N_META, hd), F32)
    km = jnp.concatenate([km_ref[...], zpad], axis=0).astype(BF16)
    vm_t = jnp.concatenate([vm_ref[...], zpad], axis=0).T.astype(BF16)
    def diag_vis(krow):
        return krow < lax.broadcasted_iota(jnp.int32, krow.shape, 1)

    def meta_vis(krow):
        return krow < N_META

    def q_tile(i, carry):
        q0 = pl.multiple_of(i * tq, tq)
        qt = q_ref[pl.ds(q0, tq), :]
        run = jnp.zeros((1, tq), F32)
        s_t = _dot_nt(kb_ref[pl.ds(q0, tq), :], qt)
        w, run = _sb_tile(s_t, bias, scale, diag_vis, run)
        o_t = _dot(vt_ref[:, pl.ds(q0, tq)], w.astype(BF16))

        def k_tile(jj, c2):
            o_acc, run2 = c2
            k0 = pl.multiple_of((i - 1 - jj) * tq, tq)
            s2 = _dot_nt(kb_ref[pl.ds(k0, tq), :], qt)
            w2, run2 = _sb_tile(s2, bias, scale, None, run2)
            return o_acc + _dot(vt_ref[:, pl.ds(k0, tq)], w2.astype(BF16)), run2

        o_t, run = lax.fori_loop(0, i, k_tile, (o_t, run))
        s3 = _dot_nt(km, qt)
        w3, run = _sb_tile(s3, bias, scale, meta_vis, run)
        o_t = o_t + _dot(vm_t, w3.astype(BF16))
        o_ref[pl.ds(q0, tq), :] = o_t.T.astype(o_ref.dtype)
        return carry

    lax.fori_loop(0, n_t, q_tile, 0)

    qm = jnp.concatenate([qm_ref[...], jnp.zeros((LANES - N_META, hd), qm_ref.dtype)], axis=0)
    s_m = _dot_nt(km, qm)
    w_m, _ = _sb_tile(s_m, bias, scale, diag_vis, jnp.zeros((1, LANES), F32))
    o_m = _dot(vm_t, w_m.astype(BF16)).T
    om_ref[...] = o_m[0:N_META, :].astype(om_ref.dtype)


def _sb_prompt(sb_bias, q_src, q_col0, kv, n_batch, seq, n_heads, hd):
    tq = 256 if seq % 256 == 0 else SSD_CHUNK
    qb = q_col0 // hd
    mb0 = n_batch * seq // N_META
    kern = functools.partial(_sb_prompt_kernel, seq=seq, tq=tq, scale=hd ** -0.5)
    return pl.pallas_call(
        kern,
        grid=(n_batch, n_heads),
        in_specs=[pl.BlockSpec(memory_space=pltpu.SMEM),
                  pl.BlockSpec((seq, hd), lambda b, h: (b, qb + h)),
                  pl.BlockSpec((N_META, hd), lambda b, h: (mb0 + b, qb + h)),
                  pl.BlockSpec((seq, hd), lambda b, h: (b, h)),
                  pl.BlockSpec((N_META, hd), lambda b, h: (mb0 + b, h)),
                  pl.BlockSpec((seq, hd), lambda b, h: (b, n_heads + h)),
                  pl.BlockSpec((N_META, hd), lambda b, h: (mb0 + b, n_heads + h))],
        out_specs=[pl.BlockSpec((seq, hd), lambda b, h: (b, h)),
                   pl.BlockSpec((N_META, hd), lambda b, h: (b, h))],
        out_shape=[jax.ShapeDtypeStruct((n_batch * seq, n_heads * hd), BF16),
                   jax.ShapeDtypeStruct((n_batch * N_META, n_heads * hd), BF16)],
        scratch_shapes=[pltpu.VMEM((seq, hd), BF16), pltpu.VMEM((hd, seq), BF16)],
        compiler_params=_cparams(("parallel", "parallel")),
        name="sb_prompt",
    )(sb_bias, q_src, q_src, kv, kv, kv, kv)


def _sb_sample_kernel(pt_ref, q_ref, kn_ref, vn_ref, bias_ref, kc_ref, vc_ref, o_ref, qbd_ref, acc_ref, run_ref,
                      *, n_heads, hd, dseq, page, scale):
    p = pl.program_id(1)
    width = n_heads * hd
    nq = n_heads * dseq
    bias = bias_ref[...]
    def new_vis(krow):
        return krow < lax.broadcasted_iota(jnp.int32, krow.shape, 1) % dseq

    def attend(k_blk, v_blk, visible):
        s_t = _dot_nt(k_blk, qbd_ref[...])
        w, run = _sb_tile(s_t, bias, scale, visible, run_ref[...])
        run_ref[...] = run
        acc_ref[...] += _dot(w.T.astype(BF16), v_blk)

    @pl.when(p == 0)
    def _():
        q_t = jnp.concatenate([q_ref[...]] * n_heads, axis=0)
        r = lax.broadcasted_iota(jnp.int32, (nq, width), 0) // dseq
        c = lax.broadcasted_iota(jnp.int32, (nq, width), 1) // hd
        qbd_ref[...] = jnp.where(r == c, q_t, 0.0).astype(BF16)
        acc_ref[...] = jnp.zeros_like(acc_ref)
        run_ref[...] = jnp.zeros_like(run_ref)
        zpad = jnp.zeros((page - dseq, width), F32)
        k_new = jnp.concatenate([kn_ref[...], zpad], axis=0).astype(BF16)
        v_new = jnp.concatenate([vn_ref[...], zpad], axis=0).astype(BF16)
        attend(k_new, v_new, new_vis)

    k_pg = jnp.concatenate([kc_ref[pl.ds(hh, page, stride=n_heads), :] for hh in range(n_heads)], axis=1)
    v_pg = jnp.concatenate([vc_ref[pl.ds(hh, page, stride=n_heads), :] for hh in range(n_heads)], axis=1)
    attend(k_pg.astype(BF16), v_pg.astype(BF16), None)

    @pl.when(p == pl.num_programs(1) - 1)
    def _():
        o_ref[...] = jnp.concatenate(
            [acc_ref[hh * dseq:(hh + 1) * dseq, hh * hd:(hh + 1) * hd] for hh in range(n_heads)], axis=1)


def _sb_sample(page_table, q, k_new, v_new, bias_lane, cache_k, cache_v, layer, dseq):
    n_db, n_pages = page_table.shape
    hd = cache_k.shape[-1]
    width = q.shape[1]
    n_heads = width // hd
    page = cache_k.shape[2] // n_heads
    nq = n_heads * dseq
    assert nq <= LANES and dseq == SUBLANES, (n_heads, dseq)
    kern = functools.partial(_sb_sample_kernel, n_heads=n_heads, hd=hd, dseq=dseq, page=page, scale=hd ** -0.5)
    cache_spec = pl.BlockSpec((None, None, page * n_heads, hd),
                              lambda b, p, pt: (pt[b, n_pages - 1 - p], layer, 0, 0))
    grid_spec = pltpu.PrefetchScalarGridSpec(
        num_scalar_prefetch=1,
        grid=(n_db, n_pages),
        in_specs=[pl.BlockSpec((dseq, width), lambda b, p, pt: (b, 0)),
                  pl.BlockSpec((dseq, width), lambda b, p, pt: (b, 0)),
                  pl.BlockSpec((dseq, width), lambda b, p, pt: (b, 0)),
                  pl.BlockSpec((1, nq), lambda b, p, pt: (0, 0)),
                  cache_spec, cache_spec],
        out_specs=pl.BlockSpec((dseq, width), lambda b, p, pt: (b, 0)),
        scratch_shapes=[pltpu.VMEM((nq, width), BF16), pltpu.VMEM((nq, width), F32), pltpu.VMEM((1, nq), F32)],
    )
    return pl.pallas_call(
        kern,
        grid_spec=grid_spec,
        out_shape=jax.ShapeDtypeStruct((n_db * dseq, width), F32),
        compiler_params=_cparams(("parallel", "arbitrary")),
        name="sb_sample",
    )(page_table, q, k_new, v_new, bias_lane, cache_k, cache_v)


def _merge_kernel(y_ref, o_ref, w1_ref, w2_ref, g1_ref, g2_ref, out_ref):
    a = _dot(y_ref[...], w1_ref[...])
    b = _dot(o_ref[...], w2_ref[...])
    out_ref[...] = (_sigmoid(g1_ref[...].astype(F32)) * a + _sigmoid(g2_ref[...].astype(F32)) * b).astype(BF16)


def _merge(y, o, w1, w2, g_src, g1_col0, g2_col0):
    n, d1 = y.shape
    d2 = o.shape[1]
    m = w1.shape[1]
    tm = _largest_tile(n, 640, 16)
    tn = _largest_tile(math.gcd(math.gcd(m, g1_col0), g2_col0), 512, LANES)
    b1, b2 = g1_col0 // tn, g2_col0 // tn
    return pl.pallas_call(
        _merge_kernel,
        grid=(n // tm, m // tn),
        in_specs=[pl.BlockSpec((tm, d1), lambda i, j: (i, 0)),
                  pl.BlockSpec((tm, d2), lambda i, j: (i, 0)),
                  pl.BlockSpec((d1, tn), lambda i, j: (0, j)),
                  pl.BlockSpec((d2, tn), lambda i, j: (0, j)),
                  pl.BlockSpec((tm, tn), lambda i, j: (i, b1 + j)),
                  pl.BlockSpec((tm, tn), lambda i, j: (i, b2 + j))],
        out_specs=pl.BlockSpec((tm, tn), lambda i, j: (i, j)),
        out_shape=jax.ShapeDtypeStruct((n, m), BF16),
        compiler_params=_cparams(("parallel", "parallel")),
        name="merge",
    )(y, o, w1, w2, g_src, g_src)


def _mixln_kernel(m_ref, wo_ref, h_ref, g_ref, b_ref, wrh_ref, wrl_ref, br_ref, ht_ref, ids_ref, ew_ref,
                  acc_ref, *, alpha, tn, n_groups, per_group):
    j = pl.program_id(1)
    acc_ref[:, pl.ds(pl.multiple_of(j * tn, tn), tn)] = _dot(m_ref[...], wo_ref[...])

    @pl.when(j == pl.num_programs(1) - 1)
    def _():
        y = _layernorm(alpha * h_ref[...] + acc_ref[...], g_ref[...], b_ref[...])
        for s in range(ht_ref.shape[0]):
            ht_ref[s] = y[:, s * LANES:(s + 1) * LANES]
        y_hi, y_lo = _split2(y)
        logits = _dot(y_hi, wrh_ref[...]) + _dot(y_lo, wrh_ref[...]) + _dot(y_hi, wrl_ref[...]) + br_ref[...]
        lane = lax.broadcasted_iota(jnp.int32, logits.shape, 1)
        n_exp = n_groups * per_group
        glog = jnp.where(lane < n_groups, logits, NEG_BIG)
        gmax = jnp.max(glog, axis=-1, keepdims=True)
        gsum = jnp.sum(jnp.exp(glog - gmax), axis=-1, keepdims=True)
        g_val = 1.0 / gsum
        g_idx = jnp.min(jnp.where(glog == gmax, lane, LANES), axis=-1, keepdims=True)
        e_lane = lane - n_groups
        in_grp = (e_lane >= g_idx * per_group) & (e_lane < (g_idx + 1) * per_group) & (e_lane < n_exp)
        sel = jnp.where(in_grp, logits, NEG_BIG)
        m1 = jnp.max(sel, axis=-1, keepdims=True)
        i1 = jnp.min(jnp.where(sel == m1, lane, LANES), axis=-1, keepdims=True)
        sel2 = jnp.where(lane == i1, NEG_BIG, sel)
        m2 = jnp.max(sel2, axis=-1, keepdims=True)
        i2 = jnp.min(jnp.where(sel2 == m2, lane, LANES), axis=-1, keepdims=True)
        e2 = jnp.exp(m2 - m1)
        w1 = g_val / (1.0 + e2)
        w2 = g_val * e2 / (1.0 + e2)
        ids_ref[...] = jnp.where(lane == 0, i1 - n_groups, jnp.where(lane == 1, i2 - n_groups, 0))
        ew_ref[...] = jnp.where(lane == 0, w1, jnp.where(lane == 1, w2, 0.0))


def _mixln(merged, w_o, h, g, b, wr_hi, wr_lo, br, alpha, n_groups, per_group):
    n, d = h.shape
    tm = _largest_tile(n, 320, 16)
    tn = _largest_tile(d, 512, LANES)
    kern = functools.partial(_mixln_kernel, alpha=alpha, tn=tn, n_groups=n_groups, per_group=per_group)
    row = lambda i, j: (i, 0)
    const = lambda i, j: (0, 0)
    return pl.pallas_call(
        kern,
        grid=(n // tm, d // tn),
        in_specs=[pl.BlockSpec((tm, d), row),
                  pl.BlockSpec((d, tn), lambda i, j: (0, j)),
                  pl.BlockSpec((tm, d), row),
                  pl.BlockSpec((1, d), const), pl.BlockSpec((1, d), const),
                  pl.BlockSpec((d, LANES), const), pl.BlockSpec((d, LANES), const), pl.BlockSpec((1, LANES), const)],
        out_specs=[pl.BlockSpec((d // LANES, tm, LANES), lambda i, j: (0, i, 0)),
                   pl.BlockSpec((tm, LANES), row), pl.BlockSpec((tm, LANES), row)],
        out_shape=[jax.ShapeDtypeStruct((d // LANES, n, LANES), F32),
                   jax.ShapeDtypeStruct((n, LANES), jnp.int32),
                   jax.ShapeDtypeStruct((n, LANES), F32)],
        scratch_shapes=[pltpu.VMEM((tm, d), F32)],
        compiler_params=_cparams(("parallel", "arbitrary")),
        name="mix_ln_router",
    )(merged, w_o, h, g.reshape(1, d), b.reshape(1, d), wr_hi, wr_lo, br)


def _moe_kernel(nused_ref, tok_ref, tokn_ref, dst_ref, gw_ref, ht_hbm, wg_ref, wu_ref, wd_ref, out_hbm,
                xbuf, a_ref, stage, gsem, ssem, *, tm, n_tok):
    i = pl.program_id(0)
    j = pl.program_id(1)
    nf = pl.num_programs(1)
    nused = nused_ref[0]
    n_chunks = xbuf.shape[0]

    def gather_copy(t, r):
        return pltpu.make_async_copy(ht_hbm.at[:, t, :], xbuf.at[:, r, :], gsem)

    def scatter_copy(r, d):
        slot = d % 2
        return pltpu.make_async_copy(stage.at[:, r, :], out_hbm.at[pl.ds(slot * n_chunks, n_chunks), d // 2, :], ssem)

    def start_gather(idx_ref):
        def body(r, c):
            gather_copy(idx_ref[0, 0, r], r).start()
            return c
        lax.fori_loop(0, tm, body, 0)

    @pl.when((i < nused) & (j == 0))
    def _():
        @pl.when(i == 0)
        def _():
            start_gather(tok_ref)

        def wait_body(r, c):
            gather_copy(0, r).wait()
            return c
        lax.fori_loop(0, tm, wait_body, 0)
        a_ref[...] = jnp.concatenate([xbuf[s] for s in range(n_chunks)], axis=1).astype(BF16)

        @pl.when(i + 1 < nused)
        def _():
            start_gather(tokn_ref)

    @pl.when(i < nused)
    def _():
        a = a_ref[...]
        hg = _dot(a, wg_ref[...])
        hu = _dot(a, wu_ref[...])
        hid = (hg * _sigmoid(hg) * hu * gw_ref[...]).astype(BF16)
        cw = 4 * LANES
        for c0 in range(0, wd_ref.shape[1], cw):
            yc = _dot(hid, wd_ref[:, c0:c0 + cw])

            @pl.when(j == 0)
            def _():
                for s in range(cw // LANES):
                    stage[c0 // LANES + s] = yc[:, s * LANES:(s + 1) * LANES]

            @pl.when(j > 0)
            def _():
                for s in range(cw // LANES):
                    stage[c0 // LANES + s] += yc[:, s * LANES:(s + 1) * LANES]

    @pl.when((i < nused) & (j == nf - 1))
    def _():
        def send(r, c):
            d = dst_ref[0, 0, r]

            @pl.when(d >= 0)
            def _():
                scatter_copy(r, d).start()
            return c
        lax.fori_loop(0, tm, send, 0)

        def drain(r, c):
            d = dst_ref[0, 0, r]

            @pl.when(d >= 0)
            def _():
                scatter_copy(r, d).wait()
            return c
        lax.fori_loop(0, tm, drain, 0)


def _moe(ht, row_tok, row_dst, row_gw, tile_e, nused, wg, wu, wd, tm):
    n_chunks, n_tok, _ = ht.shape
    d = n_chunks * LANES
    n_tiles = row_tok.shape[0] // tm
    f = wg.shape[2]
    tf = _largest_tile(f, 256, LANES)
    kern = functools.partial(_moe_kernel, tm=tm, n_tok=n_tok)
    idx3 = row_tok.reshape(n_tiles, 1, tm)
    dst3 = row_dst.reshape(n_tiles, 1, tm)
    smem_blk = lambda im: pl.BlockSpec((1, 1, tm), im, memory_space=pltpu.SMEM)
    grid_spec = pltpu.PrefetchScalarGridSpec(
        num_scalar_prefetch=2,
        grid=(n_tiles, f // tf),
        in_specs=[smem_blk(lambda i, j, te, nu: (i, 0, 0)),
                  smem_blk(lambda i, j, te, nu: (jnp.minimum(i + 1, n_tiles - 1), 0, 0)),
                  smem_blk(lambda i, j, te, nu: (i, 0, 0)),
                  pl.BlockSpec((tm, 1), lambda i, j, te, nu: (i, 0)),
                  pl.BlockSpec(memory_space=pl.ANY),
                  pl.BlockSpec((None, d, tf), lambda i, j, te, nu: (te[i], 0, j)),
                  pl.BlockSpec((None, d, tf), lambda i, j, te, nu: (te[i], 0, j)),
                  pl.BlockSpec((None, tf, d), lambda i, j, te, nu: (te[i], j, 0))],
        out_specs=pl.BlockSpec(memory_space=pl.ANY),
        scratch_shapes=[pltpu.VMEM((n_chunks, tm, LANES), F32), pltpu.VMEM((tm, d), BF16),
                        pltpu.VMEM((n_chunks, tm, LANES), F32),
                        pltpu.SemaphoreType.DMA(()), pltpu.SemaphoreType.DMA(())],
    )

    def wrapped(te_ref, nu_ref, *rest):
        kern(nu_ref, *rest)

    return pl.pallas_call(
        wrapped,
        grid_spec=grid_spec,
        out_shape=jax.ShapeDtypeStruct((2 * n_chunks, n_tok, LANES), F32),
        compiler_params=_cparams(("arbitrary", "arbitrary")),
        name="moe_ffn",
    )(tile_e, nused, idx3, idx3, dst3, row_gw.reshape(-1, 1), ht, wg, wu, wd)


def _route_plan(ids, ew, n_experts, tm, n_tiles):
    n = ids.shape[0]
    ea = ids.reshape(-1)
    onehot = (ea[:, None] == jnp.arange(n_experts, dtype=jnp.int32)[None, :]).astype(jnp.int32)
    cs = jnp.cumsum(onehot, axis=0)
    rank = jnp.take_along_axis(cs, ea[:, None], axis=1)[:, 0] - 1
    cnt = cs[-1]
    pcnt = ((cnt + tm - 1) // tm) * tm
    ends = jnp.cumsum(pcnt)
    starts = ends - pcnt
    pos = starts[ea] + rank
    p_rows = n_tiles * tm
    a_idx = jnp.arange(2 * n, dtype=jnp.int32)
    row_tok = jnp.zeros((p_rows,), jnp.int32).at[pos].set(a_idx // 2)
    row_dst = jnp.full((p_rows,), -1, jnp.int32).at[pos].set(a_idx)
    row_gw = jnp.zeros((p_rows,), F32).at[pos].set(ew.reshape(-1))
    nused = (ends[-1] // tm).astype(jnp.int32)
    tile_start = jnp.arange(n_tiles, dtype=jnp.int32) * tm
    tile_e = jnp.searchsorted(ends, tile_start, side="right").astype(jnp.int32)
    tile_e = jnp.minimum(tile_e, n_experts - 1)
    last_e = tile_e[jnp.maximum(nused - 1, 0)]
    tile_e = jnp.where(jnp.arange(n_tiles) < nused, tile_e, last_e)
    return row_tok, row_dst, row_gw, tile_e, nused.reshape(1)


def _ln2_kernel(ht_ref, mt_ref, g_ref, b_ref, of_ref, ob_ref, *, alpha):
    n_chunks = ht_ref.shape[0]
    x = jnp.concatenate([alpha * ht_ref[s] + (mt_ref[s] + mt_ref[n_chunks + s]) for s in range(n_chunks)], axis=1)
    y = _layernorm(x, g_ref[...], b_ref[...])
    of_ref[...] = y
    ob_ref[...] = y.astype(BF16)


def _ln2(h_t, moe_t, g, b, alpha):
    n_chunks, n, _ = h_t.shape
    d = n_chunks * LANES
    tm = _largest_tile(n, 160, 16)
    return pl.pallas_call(
        functools.partial(_ln2_kernel, alpha=alpha),
        grid=(n // tm,),
        in_specs=[pl.BlockSpec((n_chunks, tm, LANES), lambda i: (0, i, 0)),
                  pl.BlockSpec((moe_t.shape[0], tm, LANES), lambda i: (0, i, 0)),
                  pl.BlockSpec((1, d), lambda i: (0, 0)), pl.BlockSpec((1, d), lambda i: (0, 0))],
        out_specs=[pl.BlockSpec((tm, d), lambda i: (i, 0)), pl.BlockSpec((tm, d), lambda i: (i, 0))],
        out_shape=[jax.ShapeDtypeStruct((n, d), F32), jax.ShapeDtypeStruct((n, d), BF16)],
        compiler_params=_cparams(("parallel",)),
        name="ln2",
    )(h_t, moe_t, g.reshape(1, d), b.reshape(1, d))


def kernel(x_prompt, x_sample, cache_k, cache_v, page_table, state_ssm, state_conv, meta_tokens, ln_emb_g, ln_emb_b,
           w_in, conv_w, conv_b, dt_bias, a_log, d_skip, ssd_norm_g, w_ssd_out, w_sb_out, sb_bias, w_o, ln1_g, ln1_b,
           wr_group, br_group, wr_expert, br_expert, w_gate, w_up, w_down, ln2_g, ln2_b):
    n_b, seq, d = x_prompt.shape
    n_db, dseq, _ = x_sample.shape
    depth = w_in.shape[0]
    c_dim = conv_w.shape[-1]
    n_ssd_heads = dt_bias.shape[-1]
    p_dim = d // n_ssd_heads
    nst = (c_dim - d) // (2 * N_SSD_GROUPS)
    sb_heads = sb_bias.shape[-1]
    hd = cache_k.shape[-1]
    sbw = sb_heads * hd
    page = cache_k.shape[2]
    n_exp = w_gate.shape[1]
    per_group = n_exp // N_ROUTE_GROUPS
    alpha = float((2 * depth) ** 0.25)
    n_real = n_b * seq
    n_meta = n_b * N_META
    n_samp = n_db * dseq
    n_tok = n_real + n_meta + n_samp
    assert seq % SSD_CHUNK == 0 and n_ssd_heads <= LANES and N_ROUTE_GROUPS + n_exp <= LANES and d % 512 == 0

    sizes = (d, c_dim, n_ssd_heads, sbw, sbw, sbw, d, d)
    offs = [0]
    for s in sizes:
        offs.append(offs[-1] + s)
    o_z, o_xbc, o_dt, o_q, o_k, o_v, o_gs, o_gb = offs[:8]
    q_col0, gs_col0, gb_col0 = d, d + sbw, 2 * d + sbw

    moe_tm = 512 if n_tok >= 4096 else 64
    n_tiles = (2 * n_tok + n_exp * (moe_tm - 1)) // moe_tm + 1

    x_all = jnp.concatenate([x_prompt.reshape(n_real, d),
                             jnp.broadcast_to(meta_tokens[None].astype(x_prompt.dtype), (n_b, N_META, d)).reshape(n_meta, d),
                             x_sample.reshape(n_samp, d)], axis=0)
    h, h_bf = _embed_ln(x_all, ln_emb_g, ln_emb_b)

    lane_pad = lambda v: jnp.pad(v.astype(F32), (0, LANES - v.shape[0])).reshape(1, LANES)
    eexp = (jnp.arange(LANES)[:, None] == (jnp.arange(d)[None, :] // p_dim)).astype(BF16)
    cache_k4 = cache_k.reshape(cache_k.shape[0], depth, page * sb_heads, hd)
    cache_v4 = cache_v.reshape(cache_v.shape[0], depth, page * sb_heads, hd)

    outs = {k: [] for k in ("kp", "vp", "sp", "cp", "ks", "vs", "ss", "cs")}
    for l in range(depth):
        wl = w_in[l]
        w_bf = jnp.concatenate([wl[:, o_z:o_z + d], wl[:, o_q:o_q + sbw], wl[:, o_gs:o_gs + d], wl[:, o_gb:o_gb + d]],
                               axis=1).astype(BF16)
        w_xbc = wl[:, o_xbc:o_xbc + c_dim].astype(BF16)
        w_kv = wl[:, o_k:o_k + 2 * sbw].astype(BF16)
        w_dt = jnp.pad(wl[:, o_dt:o_dt + n_ssd_heads], ((0, 0), (0, LANES - n_ssd_heads))).astype(BF16)

        u_bf = _matmul(h_bf, w_bf, BF16, "in_proj_bf16")
        u_xbc = _matmul(h_bf, w_xbc, F32, "in_proj_xbc")
        u_kv = _matmul(h_bf, w_kv, F32, "in_proj_kv")
        u_dt = _matmul(h_bf, w_dt, F32, "in_proj_dt")

        consts = (jnp.pad(conv_w[l], ((0, SUBLANES - conv_w.shape[1]), (0, 0))), conv_b[l].reshape(1, c_dim),
                  lane_pad(dt_bias[l]), lane_pad(-jnp.exp(a_log[l].astype(F32))),
                  jnp.repeat(d_skip[l], p_dim).reshape(1, d), ssd_norm_g[l].reshape(1, d), eexp)
        xm = u_xbc[n_real:n_real + n_meta]
        y_meta, st_meta = _ssd_small(xm, jnp.zeros((n_b, SUBLANES, c_dim), F32),
                                     u_bf[n_real:n_real + n_meta, 0:d].astype(F32), u_dt[n_real:n_real + n_meta],
                                     jnp.zeros((n_b, nst, d), F32), consts, n_b, N_META, d, n_ssd_heads, nst, "ssd_meta")
        y_real, st_real = _ssd_main(u_xbc, u_bf, u_dt, st_meta, consts, n_b, seq, d, n_ssd_heads, nst)
        xs_ = u_xbc[n_real + n_meta:]
        prev_s = jnp.pad(state_conv[l], ((0, 0), (SUBLANES - state_conv.shape[2], 0), (0, 0)))
        st0_s = jnp.transpose(state_ssm[l].astype(F32), (0, 3, 1, 2)).reshape(n_db, nst, d)
        y_samp, st_samp = _ssd_small(xs_, prev_s, u_bf[n_real + n_meta:, 0:d].astype(F32), u_dt[n_real + n_meta:],
                                     st0_s, consts, n_db, dseq, d, n_ssd_heads, nst, "ssd_sample")
        y_ssd = jnp.concatenate([y_real, y_meta.astype(BF16), y_samp.astype(BF16)], axis=0)

        o_real, o_meta = _sb_prompt(sb_bias[l].astype(F32), u_bf, q_col0, u_kv, n_b, seq, sb_heads, hd)
        q_s = u_bf[n_real + n_meta:, q_col0:q_col0 + sbw].astype(F32)
        kv_s = u_kv[n_real + n_meta:]
        o_samp = _sb_sample(page_table, q_s, kv_s[:, :sbw], kv_s[:, sbw:],
                            jnp.repeat(sb_bias[l].astype(F32), dseq).reshape(1, sb_heads * dseq),
                            cache_k4, cache_v4, l, dseq)
        o_sb = jnp.concatenate([o_real, o_meta, o_samp.astype(BF16)], axis=0)

        merged = _merge(y_ssd, o_sb, w_ssd_out[l].astype(BF16), w_sb_out[l].astype(BF16), u_bf, gs_col0, gb_col0)
        wr = jnp.concatenate([wr_group[l], jnp.transpose(wr_expert[l], (1, 0, 2)).reshape(d, n_exp)], axis=1)
        wr = jnp.pad(wr.astype(F32), ((0, 0), (0, LANES - wr.shape[1])))
        wr_hi = wr.astype(BF16)
        wr_lo = (wr - wr_hi.astype(F32)).astype(BF16)
        br = lane_pad(jnp.concatenate([br_group[l], br_expert[l].reshape(-1)]))
        h1_t, ids, ew = _mixln(merged, w_o[l].astype(BF16), h, ln1_g[l], ln1_b[l], wr_hi, wr_lo, br, alpha,
                               N_ROUTE_GROUPS, per_group)
        plan = _route_plan(ids[:, :2], ew[:, :2], n_exp, moe_tm, n_tiles)
        moe_t = _moe(h1_t, *plan, w_gate[l].astype(BF16), w_up[l].astype(BF16), w_down[l].astype(BF16), moe_tm)
        h, h_bf = _ln2(h1_t, moe_t, ln2_g[l], ln2_b[l], alpha)

        def kv_out(col0):
            real = u_kv[:n_real, col0:col0 + sbw].reshape(n_b, seq, sb_heads, hd)
            meta = u_kv[n_real:n_real + n_meta, col0:col0 + sbw].reshape(n_b, N_META, sb_heads, hd)
            samp = u_kv[n_real + n_meta:, col0:col0 + sbw].reshape(n_db, dseq, sb_heads, hd)
            return jnp.concatenate([meta, real], axis=1), samp
        kp, ks = kv_out(0)
        vp, vs = kv_out(sbw)
        outs["kp"].append(kp); outs["ks"].append(ks); outs["vp"].append(vp); outs["vs"].append(vs)
        to_state = lambda st, nb: jnp.transpose(st.reshape(nb, nst, n_ssd_heads, p_dim), (0, 2, 3, 1))
        outs["sp"].append(to_state(st_real, n_b)); outs["ss"].append(to_state(st_samp, n_db))
        outs["cp"].append(u_xbc[:n_real].reshape(n_b, seq, c_dim)[:, seq - 3:, :])
        outs["cs"].append(xs_.reshape(n_db, dseq, c_dim)[:, dseq - 3:, :])

    st = lambda k: jnp.stack(outs[k])
    return (h[:n_real].reshape(n_b, seq, d), h[n_real + n_meta:].reshape(n_db, dseq, d),
            st("kp"), st("vp"), st("sp"), st("cp"), st("ks"), st("vs"), st("ss"), st("cs"))
```

```python
import functools
import math

import jax
import jax.numpy as jnp
from jax import lax
from jax.experimental import pallas as pl
from jax.experimental.pallas import tpu as pltpu

F32 = jnp.float32
BF16 = jnp.bfloat16

N_META = 16
SSD_CHUNK = 128
N_SSD_GROUPS = 8
N_ROUTE_GROUPS = 4
LN_EPS = 1e-5
RMS_EPS = 1e-5
NEG_BIG = -1e30

LANES = 128
SUBLANES = 8
VMEM_LIMIT_BYTES = 56 * 1024 * 1024
MOE_VMEM_LIMIT_BYTES = 60 * 1024 * 1024


def _cparams(sem):
    return pltpu.CompilerParams(dimension_semantics=sem, vmem_limit_bytes=VMEM_LIMIT_BYTES)


def _largest_tile(n, cap, mult):
    best = None
    for t in range(mult, min(n, cap) + 1, mult):
        if n % t == 0:
            best = t
    assert best is not None, (n, cap, mult)
    return best


def _dot(a, b):
    return jnp.dot(a, b, preferred_element_type=F32)


def _dot_nt(a, b):
    return lax.dot_general(a, b, (((1,), (1,)), ((), ())), preferred_element_type=F32)


def _dot_tn(a, b):
    return lax.dot_general(a, b, (((0,), (0,)), ((), ())), preferred_element_type=F32)


def _split3(x):
    hi = x.astype(BF16)
    r = x - hi.astype(F32)
    mid = r.astype(BF16)
    lo = (r - mid.astype(F32)).astype(BF16)
    return hi, mid, lo


def _split2(x):
    hi = x.astype(BF16)
    lo = (x - hi.astype(F32)).astype(BF16)
    return hi, lo


def _softplus(x):
    return jnp.maximum(x, 0.0) + jnp.log(1.0 + jnp.exp(-jnp.abs(x)))


def _sigmoid(x):
    return 1.0 / (1.0 + jnp.exp(-x))


def _layernorm(x, g, b):
    mu = jnp.mean(x, axis=-1, keepdims=True)
    xc = x - mu
    var = jnp.mean(xc * xc, axis=-1, keepdims=True)
    return xc * lax.rsqrt(var + LN_EPS) * g + b


def _ln_kernel(x_ref, g_ref, b_ref, of_ref, ob_ref):
    y = _layernorm(x_ref[...], g_ref[...], b_ref[...])
    of_ref[...] = y
    ob_ref[...] = y.astype(BF16)


def _embed_ln(x, g, b):
    n, d = x.shape
    tm = _largest_tile(n, 320, 16)
    return pl.pallas_call(
        _ln_kernel,
        grid=(n // tm,),
        in_specs=[pl.BlockSpec((tm, d), lambda i: (i, 0)),
                  pl.BlockSpec((1, d), lambda i: (0, 0)),
                  pl.BlockSpec((1, d), lambda i: (0, 0))],
        out_specs=[pl.BlockSpec((tm, d), lambda i: (i, 0)),
                   pl.BlockSpec((tm, d), lambda i: (i, 0))],
        out_shape=[jax.ShapeDtypeStruct((n, d), F32), jax.ShapeDtypeStruct((n, d), BF16)],
        compiler_params=_cparams(("parallel",)),
        name="embed_ln",
    )(x, g.reshape(1, d), b.reshape(1, d))


def _mm_kernel(x_ref, w_ref, o_ref):
    o_ref[...] = _dot(x_ref[...], w_ref[...]).astype(o_ref.dtype)


def _matmul(x, w, out_dtype, name):
    n, k = x.shape
    m = w.shape[1]
    tm = _largest_tile(n, 640, 16)
    tn = _largest_tile(m, 1024, LANES)
    return pl.pallas_call(
        _mm_kernel,
        grid=(n // tm, m // tn),
        in_specs=[pl.BlockSpec((tm, k), lambda i, j: (i, 0)),
                  pl.BlockSpec((k, tn), lambda i, j: (0, j))],
        out_specs=pl.BlockSpec((tm, tn), lambda i, j: (i, j)),
        out_shape=jax.ShapeDtypeStruct((n, m), out_dtype),
        compiler_params=_cparams(("parallel", "parallel")),
        name=name,
    )(x, w)


def _mm_into_kernel(x_ref, w_ref, buf_ref, o_ref):
    del buf_ref
    o_ref[...] = _dot(x_ref[...], w_ref[...]).astype(o_ref.dtype)


def _matmul_into_layer(x, w, buf, layer, name):
    n, k = x.shape
    m = w.shape[1]
    tm = _largest_tile(n, 640, 16)
    tn = _largest_tile(m, 1024, LANES)
    return pl.pallas_call(
        _mm_into_kernel,
        grid=(n // tm, m // tn),
        in_specs=[pl.BlockSpec((tm, k), lambda i, j: (i, 0)), pl.BlockSpec((k, tn), lambda i, j: (0, j)),
                  pl.BlockSpec(memory_space=pl.ANY)],
        out_specs=pl.BlockSpec((None, tm, tn), lambda i, j: (layer, i, j)),
        out_shape=jax.ShapeDtypeStruct(buf.shape, buf.dtype),
        input_output_aliases={2: 0},
        compiler_params=_cparams(("parallel", "parallel")),
        name=name,
    )(x, w, buf)


def _ssd_chunk(xpad_ref, act_ref, st_ref, z, dtraw, n_valid, cw_ref, cb_ref, dtb_ref, aneg_ref, dskip_ref,
               ng_ref, eexp_ref, y_write, d_model, n_heads):
    q = SSD_CHUNK
    c_dim = xpad_ref.shape[1]
    nst = st_ref.shape[0]
    g_cnt = N_SSD_GROUPS
    gw = d_model // g_cnt
    heads_per_group = n_heads // g_cnt
    b_off = d_model
    c_off = d_model + g_cnt * nst

    ct = 512
    for c0 in range(0, c_dim, ct):
        acc = cb_ref[:, c0:c0 + ct]
        for i in range(4):
            acc = acc + xpad_ref[pl.ds(SUBLANES - 3 + i, q), c0:c0 + ct] * cw_ref[i:i + 1, c0:c0 + ct]
        act_ref[:, c0:c0 + ct] = acc * _sigmoid(acc)

    rows = lax.broadcasted_iota(jnp.int32, (q, q), 0)
    cols = lax.broadcasted_iota(jnp.int32, (q, q), 1)
    tri = cols <= rows
    tril_ones = jnp.where(tri, 1.0, 0.0).astype(BF16)
    low_half = cols < (LANES // 2)

    dt = _softplus(dtraw + dtb_ref[...])
    if n_valid < q:
        dt = jnp.where(rows < n_valid, dt, 0.0)
    da = dt * aneg_ref[...]
    d_hi, d_mid, d_lo = _split3(da)
    acum = _dot(tril_ones, d_hi) + _dot(tril_ones, d_mid) + _dot(tril_ones, d_lo)
    last = acum[q - 1:q, :]
    w_end = jnp.exp(last - acum) * dt
    acum_t = acum.T
    dt_t = dt.T
    eexp = eexp_ref[...]
    w_exp = _dot(w_end.astype(BF16), eexp)
    cd = jnp.broadcast_to(jnp.exp(last), (SUBLANES, LANES))
    cd_hi, cd_lo = _split2(cd)
    cd_exp = (_dot(cd_hi, eexp) + _dot(cd_lo, eexp))[0:1, :]

    for g in range(g_cnt):
        bm = act_ref[:, b_off + g * nst:b_off + (g + 1) * nst]
        cm = act_ref[:, c_off + g * nst:c_off + (g + 1) * nst]
        bm_b = bm.astype(BF16)
        cb_mat = _dot_nt(cm.astype(BF16), bm_b)
        ys = []
        for pr in range(heads_per_group // 2):
            col0 = g * gw + pr * LANES
            lhs = []
            for hh in range(2):
                h = g * heads_per_group + 2 * pr + hh
                colb = jnp.broadcast_to(acum[:, h:h + 1], (q, q))
                seg = colb - acum_t[h:h + 1, :]
                dec = jnp.exp(jnp.where(tri, seg, NEG_BIG))
                lhs.append((cb_mat * dec * dt_t[h:h + 1, :]).astype(BF16))
                lhs.append((cm * jnp.exp(colb)).astype(BF16))
            xp = act_ref[:, col0:col0 + LANES]
            sp = st_ref[:, col0:col0 + LANES]
            rhs = jnp.concatenate([jnp.where(low_half, xp, 0.0), jnp.where(low_half, sp, 0.0),
                                   jnp.where(low_half, 0.0, xp), jnp.where(low_half, 0.0, sp)], axis=0).astype(BF16)
            yp = _dot(jnp.concatenate(lhs, axis=1), rhs)
            yp = yp + dskip_ref[:, col0:col0 + LANES] * xp
            zp = z[:, col0:col0 + LANES]
            ys.append(yp * (zp * _sigmoid(zp)))
        ss = ys[0] * ys[0]
        for yp in ys[1:]:
            ss = ss + yp * yp
        inv = lax.rsqrt(jnp.sum(ss, axis=-1, keepdims=True) / gw + RMS_EPS)
        for pr, yp in enumerate(ys):
            col0 = g * gw + pr * LANES
            y_write(col0, yp * inv * ng_ref[:, col0:col0 + LANES])
        xg = act_ref[:, g * gw:(g + 1) * gw]
        xw = (xg * w_exp[:, g * gw:(g + 1) * gw]).astype(BF16)
        dst = _dot_tn(bm_b, xw)
        st_ref[:, g * gw:(g + 1) * gw] = st_ref[:, g * gw:(g + 1) * gw] * cd_exp[:, g * gw:(g + 1) * gw] + dst


def _ssd_main_kernel(xbc_ref, xmeta_ref, z_ref, dt_ref, st0_ref, cw_ref, cb_ref, dtb_ref, aneg_ref, dskip_ref,
                     ng_ref, eexp_ref, y_ref, stout_ref, xpad_ref, act_ref, st_ref, *, d_model, n_heads):
    c = pl.program_id(1)

    @pl.when(c == 0)
    def _():
        st_ref[...] = st0_ref[0]
        xpad_ref[0:SUBLANES, :] = xmeta_ref[N_META - SUBLANES:N_META, :]

    xpad_ref[SUBLANES:SUBLANES + SSD_CHUNK, :] = xbc_ref[...]

    def y_write(col0, val):
        y_ref[:, col0:col0 + val.shape[1]] = val.astype(y_ref.dtype)

    _ssd_chunk(xpad_ref, act_ref, st_ref, z_ref[...].astype(F32), dt_ref[...], SSD_CHUNK, cw_ref, cb_ref, dtb_ref,
               aneg_ref, dskip_ref, ng_ref, eexp_ref, y_write, d_model, n_heads)
    xpad_ref[0:SUBLANES, :] = xbc_ref[SSD_CHUNK - SUBLANES:SSD_CHUNK, :]

    @pl.when(c == pl.num_programs(1) - 1)
    def _():
        stout_ref[0] = st_ref[...]


def _ssd_small_kernel(x_ref, prev_ref, z_ref, dt_ref, st0_ref, cw_ref, cb_ref, dtb_ref, aneg_ref, dskip_ref,
                      ng_ref, eexp_ref, y_ref, stout_ref, xpad_ref, act_ref, st_ref, *, d_model, n_heads, seq):
    st_ref[...] = st0_ref[0]
    xpad_ref[0:SUBLANES, :] = prev_ref[0]
    xpad_ref[SUBLANES:SUBLANES + seq, :] = x_ref[...]
    xpad_ref[SUBLANES + seq:, :] = jnp.zeros((SSD_CHUNK - seq, xpad_ref.shape[1]), F32)
    pad = jnp.zeros((SSD_CHUNK - seq, d_model), F32)
    z = jnp.concatenate([z_ref[...], pad], axis=0)
    dtr = jnp.concatenate([dt_ref[...], jnp.zeros((SSD_CHUNK - seq, LANES), F32)], axis=0)

    def y_write(col0, val):
        y_ref[:, col0:col0 + val.shape[1]] = val[0:seq, :]

    _ssd_chunk(xpad_ref, act_ref, st_ref, z, dtr, seq, cw_ref, cb_ref, dtb_ref, aneg_ref, dskip_ref, ng_ref,
               eexp_ref, y_write, d_model, n_heads)
    stout_ref[0] = st_ref[...]


def _ssd_consts_specs(c_dim, d_model, idx):
    return [pl.BlockSpec((SUBLANES, c_dim), idx), pl.BlockSpec((1, c_dim), idx), pl.BlockSpec((1, LANES), idx),
            pl.BlockSpec((1, LANES), idx), pl.BlockSpec((1, d_model), idx), pl.BlockSpec((1, d_model), idx),
            pl.BlockSpec((LANES, d_model), idx)]


def _ssd_main(xbc, z_src, dt, st0, consts, n_batch, seq, d_model, n_heads, nst):
    c_dim = xbc.shape[1]
    nc = seq // SSD_CHUNK
    meta_blk0 = n_batch * seq // N_META
    kern = functools.partial(_ssd_main_kernel, d_model=d_model, n_heads=n_heads)
    return pl.pallas_call(
        kern,
        grid=(n_batch, nc),
        in_specs=[pl.BlockSpec((SSD_CHUNK, c_dim), lambda b, c: (b * nc + c, 0)),
                  pl.BlockSpec((N_META, c_dim), lambda b, c: (meta_blk0 + b, 0)),
                  pl.BlockSpec((SSD_CHUNK, d_model), lambda b, c: (b * nc + c, 0)),
                  pl.BlockSpec((SSD_CHUNK, LANES), lambda b, c: (b * nc + c, 0)),
                  pl.BlockSpec((1, nst, d_model), lambda b, c: (b, 0, 0))]
        + _ssd_consts_specs(c_dim, d_model, lambda b, c: (0, 0)),
        out_specs=[pl.BlockSpec((SSD_CHUNK, d_model), lambda b, c: (b * nc + c, 0)),
                   pl.BlockSpec((1, nst, d_model), lambda b, c: (b, 0, 0))],
        out_shape=[jax.ShapeDtypeStruct((n_batch * seq, d_model), BF16),
                   jax.ShapeDtypeStruct((n_batch, nst, d_model), F32)],
        scratch_shapes=[pltpu.VMEM((SUBLANES + SSD_CHUNK, c_dim), F32), pltpu.VMEM((SSD_CHUNK, c_dim), F32),
                        pltpu.VMEM((nst, d_model), F32)],
        compiler_params=_cparams(("parallel", "arbitrary")),
        name="ssd_main",
    )(xbc, xbc, z_src, dt, st0, *consts)


def _ssd_small(x, prev, z, dt, st0, consts, n_seq, seq, d_model, n_heads, nst, name):
    c_dim = x.shape[1]
    kern = functools.partial(_ssd_small_kernel, d_model=d_model, n_heads=n_heads, seq=seq)
    return pl.pallas_call(
        kern,
        grid=(n_seq,),
        in_specs=[pl.BlockSpec((seq, c_dim), lambda i: (i, 0)),
                  pl.BlockSpec((1, SUBLANES, c_dim), lambda i: (i, 0, 0)),
                  pl.BlockSpec((seq, d_model), lambda i: (i, 0)),
                  pl.BlockSpec((seq, LANES), lambda i: (i, 0)),
                  pl.BlockSpec((1, nst, d_model), lambda i: (i, 0, 0))]
        + _ssd_consts_specs(c_dim, d_model, lambda i: (0, 0)),
        out_specs=[pl.BlockSpec((seq, d_model), lambda i: (i, 0)),
                   pl.BlockSpec((1, nst, d_model), lambda i: (i, 0, 0))],
        out_shape=[jax.ShapeDtypeStruct((n_seq * seq, d_model), F32),
                   jax.ShapeDtypeStruct((n_seq, nst, d_model), F32)],
        scratch_shapes=[pltpu.VMEM((SUBLANES + SSD_CHUNK, c_dim), F32), pltpu.VMEM((SSD_CHUNK, c_dim), F32),
                        pltpu.VMEM((nst, d_model), F32)],
        compiler_params=_cparams(("parallel",)),
        name=name,
    )(x, prev, z, dt, st0, *consts)


SB_CLAMP = 80.0
SB_PAGES_PER_STEP = 4
SB_TILES_PER_ITER = 2


def _suffix_scan_matrix(nk):
    r = lax.broadcasted_iota(jnp.int32, (nk + SUBLANES, nk), 0)
    c = lax.broadcasted_iota(jnp.int32, (nk + SUBLANES, nk), 1)
    return jnp.where((c > r) | (r >= nk), 1.0, 0.0).astype(BF16)


def _sb_tile(s_t, bias, scale, visible, run, scan_mat):
    return _sb_weights(_sb_scan(_sb_logits(s_t, bias, scale, visible), scan_mat), run)


def _sb_logits(s_t, bias, scale, visible):
    z = s_t * scale + bias
    t = jnp.minimum(z, SB_CLAMP)
    lg = jnp.log(1.0 + jnp.exp(t))
    lneg = (t - z) - lg
    vis = None
    if visible is not None:
        vis = visible(lax.broadcasted_iota(jnp.int32, s_t.shape, 0), lax.broadcasted_iota(jnp.int32, s_t.shape, 1))
        lneg = jnp.where(vis, lneg, 0.0)
    return (t - lg, vis) + _split2(lneg)


def _sb_scan(logits, scan_mat):
    logsig, vis, l_hi, l_lo = logits
    return logsig, vis, _dot(scan_mat, l_hi) + _dot(scan_mat, l_lo)


def _sb_weights(scanned, run):
    logsig, vis, sc = scanned
    nk = logsig.shape[0]
    w = jnp.exp(logsig + (sc[0:nk, :] + run))
    if vis is not None:
        w = jnp.where(vis, w, 0.0)
    return w, run + sc[nk:nk + 1, :]


def _sb_prompt_kernel(bias_ref, q_ref, qm_ref, k_ref, km_ref, v_ref, vm_ref, o_ref, om_ref, kb_ref, vt_ref,
                      *, seq, tq, scale):
    h = pl.program_id(1)
    bias = bias_ref[h]
    n_t = seq // tq
    hd = k_ref.shape[1]
    for t in range(n_t):
        kb_ref[t * tq:(t + 1) * tq, :] = k_ref[t * tq:(t + 1) * tq, :].astype(BF16)
        vt_ref[:, t * tq:(t + 1) * tq] = v_ref[t * tq:(t + 1) * tq, :].T.astype(BF16)
    zpad = jnp.zeros((LANES - N_META, hd), F32)
    km = jnp.concatenate([km_ref[...], zpad], axis=0).astype(BF16)
    vm_t = jnp.concatenate([vm_ref[...], zpad], axis=0).T.astype(BF16)
    scan_t = _suffix_scan_matrix(tq)
    scan_m = _suffix_scan_matrix(LANES)

    def diag_vis(krow, qcol):
        return krow < qcol

    def meta_vis(krow, qcol):
        return krow < N_META

    def q_tile(i, carry):
        q0 = pl.multiple_of(i * tq, tq)
        qt = q_ref[pl.ds(q0, tq), :]
        lg_d = _sb_logits(_dot_nt(kb_ref[pl.ds(q0, tq), :], qt), bias, scale, diag_vis)
        lg_m = _sb_logits(_dot_nt(km, qt), bias, scale, meta_vis)
        sc_d = _sb_scan(lg_d, scan_t)
        sc_m = _sb_scan(lg_m, scan_m)
        w, run = _sb_weights(sc_d, jnp.zeros((1, tq), F32))
        o_t = _dot(vt_ref[:, pl.ds(q0, tq)], w.astype(BF16))

        def k_tiles(kts, c2):
            o_acc, run2 = c2
            k0s = [pl.multiple_of(kt * tq, tq) for kt in kts]
            logits = [_sb_logits(_dot_nt(kb_ref[pl.ds(k0, tq), :], qt), bias, scale, None) for k0 in k0s]
            scanned = [_sb_scan(lg, scan_t) for lg in logits]
            for k0, sc in zip(k0s, scanned):
                w2, run2 = _sb_weights(sc, run2)
                o_acc = o_acc + _dot(vt_ref[:, pl.ds(k0, tq)], w2.astype(BF16))
            return o_acc, run2

        g = SB_TILES_PER_ITER
        o_t, run = lax.fori_loop(0, i // g, lambda pp, c2: k_tiles([i - 1 - g * pp - u for u in range(g)], c2),
                                 (o_t, run))
        o_t, run = lax.fori_loop(0, i % g, lambda rr, c2: k_tiles([i % g - 1 - rr], c2), (o_t, run))
        w3, run = _sb_weights(sc_m, run)
        o_t = o_t + _dot(vm_t, w3.astype(BF16))
        o_ref[pl.ds(q0, tq), :] = o_t.T.astype(o_ref.dtype)
        return carry

    lax.fori_loop(0, n_t, q_tile, 0)

    qm = jnp.concatenate([qm_ref[...], jnp.zeros((LANES - N_META, hd), qm_ref.dtype)], axis=0)
    s_m = _dot_nt(km, qm)
    w_m, _ = _sb_tile(s_m, bias, scale, diag_vis, jnp.zeros((1, LANES), F32), scan_m)
    o_m = _dot(vm_t, w_m.astype(BF16)).T
    om_ref[...] = o_m[0:N_META, :].astype(om_ref.dtype)


def _sb_prompt(sb_bias, q_src, kv_all, layer, n_batch, seq, n_heads, hd):
    tq = 256 if seq % 256 == 0 else SSD_CHUNK
    mb0 = n_batch * seq // N_META
    kern = functools.partial(_sb_prompt_kernel, seq=seq, tq=tq, scale=hd ** -0.5)
    return pl.pallas_call(
        kern,
        grid=(n_batch, n_heads),
        in_specs=[pl.BlockSpec(memory_space=pltpu.SMEM),
                  pl.BlockSpec((seq, hd), lambda b, h: (b, h)),
                  pl.BlockSpec((N_META, hd), lambda b, h: (mb0 + b, h)),
                  pl.BlockSpec((None, seq, hd), lambda b, h: (layer, b, h)),
                  pl.BlockSpec((None, N_META, hd), lambda b, h: (layer, mb0 + b, h)),
                  pl.BlockSpec((None, seq, hd), lambda b, h: (layer, b, n_heads + h)),
                  pl.BlockSpec((None, N_META, hd), lambda b, h: (layer, mb0 + b, n_heads + h))],
        out_specs=[pl.BlockSpec((seq, hd), lambda b, h: (b, h)),
                   pl.BlockSpec((N_META, hd), lambda b, h: (b, h))],
        out_shape=[jax.ShapeDtypeStruct((n_batch * seq, n_heads * hd), BF16),
                   jax.ShapeDtypeStruct((n_batch * N_META, n_heads * hd), BF16)],
        scratch_shapes=[pltpu.VMEM((seq, hd), BF16), pltpu.VMEM((hd, seq), BF16)],
        compiler_params=_cparams(("parallel", "parallel")),
        name="sb_prompt",
    )(sb_bias, q_src, q_src, kv_all, kv_all, kv_all, kv_all)


def _sb_sample_kernel(pt_ref, q_ref, kn_ref, vn_ref, bias_ref, *refs, n_heads, hd, dseq, page, scale):
    cache_refs, (o_ref, qbd_ref, acc_ref, run_ref) = refs[:-4], refs[-4:]
    p = pl.program_id(1)
    width = n_heads * hd
    nq = n_heads * dseq
    bias = bias_ref[...]
    scan_p = _suffix_scan_matrix(page)

    def new_vis(krow, lane):
        return krow < lane % dseq

    def attend(k_blk, v_blk, visible, run):
        s_t = _dot_nt(k_blk, qbd_ref[...])
        w, run = _sb_tile(s_t, bias, scale, visible, run, scan_p)
        return _dot(w.T.astype(BF16), v_blk), run

    @pl.when(p == 0)
    def _():
        q_t = jnp.concatenate([q_ref[...]] * n_heads, axis=0)
        r = lax.broadcasted_iota(jnp.int32, (nq, width), 0) // dseq
        c = lax.broadcasted_iota(jnp.int32, (nq, width), 1) // hd
        qbd_ref[...] = jnp.where(r == c, q_t, 0.0).astype(BF16)
        zpad = jnp.zeros((page - dseq, width), F32)
        k_new = jnp.concatenate([kn_ref[...], zpad], axis=0).astype(BF16)
        v_new = jnp.concatenate([vn_ref[...], zpad], axis=0).astype(BF16)
        acc_ref[...], run_ref[...] = attend(k_new, v_new, new_vis, jnp.zeros(run_ref.shape, F32))

    def page_rows(ref):
        return jnp.concatenate([ref[pl.ds(hh, page, stride=n_heads), :] for hh in range(n_heads)],
                               axis=1).astype(BF16)

    n_pg = len(cache_refs) // 2
    qbd = qbd_ref[...]
    logits = [_sb_logits(_dot_nt(page_rows(kc_ref), qbd), bias, scale, None) for kc_ref in cache_refs[:n_pg]]
    scanned = [_sb_scan(lg, scan_p) for lg in logits]
    run = run_ref[...]
    contrib = None
    for sc, vc_ref in zip(scanned, cache_refs[n_pg:]):
        w, run = _sb_weights(sc, run)
        c = _dot(w.T.astype(BF16), page_rows(vc_ref))
        contrib = c if contrib is None else contrib + c
    acc_ref[...] += contrib
    run_ref[...] = run

    @pl.when(p == pl.num_programs(1) - 1)
    def _():
        o_ref[...] = jnp.concatenate(
            [acc_ref[hh * dseq:(hh + 1) * dseq, hh * hd:(hh + 1) * hd] for hh in range(n_heads)], axis=1)


def _sb_sample(page_table, q, k_new, v_new, bias_lane, cache_k, cache_v, layer, dseq):
    n_db, n_pages = page_table.shape
    hd = cache_k.shape[-1]
    width = q.shape[1]
    n_heads = width // hd
    page = cache_k.shape[2] // n_heads
    nq = n_heads * dseq
    assert nq <= LANES and dseq == SUBLANES, (n_heads, dseq)
    kern = functools.partial(_sb_sample_kernel, n_heads=n_heads, hd=hd, dseq=dseq, page=page, scale=hd ** -0.5)
    pps = _largest_tile(n_pages, SB_PAGES_PER_STEP, 1)

    def cache_spec(k):
        return pl.BlockSpec((None, None, page * n_heads, hd),
                            lambda b, p, pt: (pt[b, n_pages - 1 - (p * pps + k)], layer, 0, 0))

    grid_spec = pltpu.PrefetchScalarGridSpec(
        num_scalar_prefetch=1,
        grid=(n_db, n_pages // pps),
        in_specs=[pl.BlockSpec((dseq, width), lambda b, p, pt: (b, 0)),
                  pl.BlockSpec((dseq, width), lambda b, p, pt: (b, 0)),
                  pl.BlockSpec((dseq, width), lambda b, p, pt: (b, 0)),
                  pl.BlockSpec((1, nq), lambda b, p, pt: (0, 0))]
        + [cache_spec(k) for k in range(pps)] * 2,
        out_specs=pl.BlockSpec((dseq, width), lambda b, p, pt: (b, 0)),
        scratch_shapes=[pltpu.VMEM((nq, width), BF16), pltpu.VMEM((nq, width), F32), pltpu.VMEM((1, nq), F32)],
    )
    return pl.pallas_call(
        kern,
        grid_spec=grid_spec,
        out_shape=jax.ShapeDtypeStruct((n_db * dseq, width), F32),
        compiler_params=_cparams(("parallel", "arbitrary")),
        name="sb_sample",
    )(page_table, q, k_new, v_new, bias_lane, *([cache_k] * pps + [cache_v] * pps))


def _merge_kernel(y_ref, o_ref, w1_ref, w2_ref, g1_ref, g2_ref, out_ref):
    a = _dot(y_ref[...], w1_ref[...])
    b = _dot(o_ref[...], w2_ref[...])
    out_ref[...] = (_sigmoid(g1_ref[...].astype(F32)) * a + _sigmoid(g2_ref[...].astype(F32)) * b).astype(BF16)


def _merge(y, o, w1, w2, g_src, g1_col0, g2_col0):
    n, d1 = y.shape
    d2 = o.shape[1]
    m = w1.shape[1]
    tm = _largest_tile(n, 640, 16)
    tn = _largest_tile(math.gcd(math.gcd(m, g1_col0), g2_col0), 512, LANES)
    b1, b2 = g1_col0 // tn, g2_col0 // tn
    return pl.pallas_call(
        _merge_kernel,
        grid=(n // tm, m // tn),
        in_specs=[pl.BlockSpec((tm, d1), lambda i, j: (i, 0)),
                  pl.BlockSpec((tm, d2), lambda i, j: (i, 0)),
                  pl.BlockSpec((d1, tn), lambda i, j: (0, j)),
                  pl.BlockSpec((d2, tn), lambda i, j: (0, j)),
                  pl.BlockSpec((tm, tn), lambda i, j: (i, b1 + j)),
                  pl.BlockSpec((tm, tn), lambda i, j: (i, b2 + j))],
        out_specs=pl.BlockSpec((tm, tn), lambda i, j: (i, j)),
        out_shape=jax.ShapeDtypeStruct((n, m), BF16),
        compiler_params=_cparams(("parallel", "parallel")),
        name="merge",
    )(y, o, w1, w2, g_src, g_src)


def _mixln_kernel(m_ref, wo_ref, h_ref, g_ref, b_ref, wrh_ref, wrl_ref, br_ref, ht_ref, ids_ref, ew_ref,
                  acc_ref, *, alpha, tn, n_groups, per_group):
    j = pl.program_id(1)
    acc_ref[:, pl.ds(pl.multiple_of(j * tn, tn), tn)] = _dot(m_ref[...], wo_ref[...])

    @pl.when(j == pl.num_programs(1) - 1)
    def _():
        y = _layernorm(alpha * h_ref[...] + acc_ref[...], g_ref[...], b_ref[...])
        tm, d = y.shape
        pitch = ht_ref.shape[0] // tm
        for s in range(pitch):
            chunk = y[:, s * LANES:(s + 1) * LANES] if s < d // LANES else jnp.zeros((tm, LANES), F32)
            ht_ref[pl.ds(s, tm, stride=pitch), :] = chunk
        y_hi, y_lo = _split2(y)
        logits = _dot(y_hi, wrh_ref[...]) + _dot(y_lo, wrh_ref[...]) + _dot(y_hi, wrl_ref[...]) + br_ref[...]
        lane = lax.broadcasted_iota(jnp.int32, logits.shape, 1)
        n_exp = n_groups * per_group
        glog = jnp.where(lane < n_groups, logits, NEG_BIG)
        gmax = jnp.max(glog, axis=-1, keepdims=True)
        gsum = jnp.sum(jnp.exp(glog - gmax), axis=-1, keepdims=True)
        g_val = 1.0 / gsum
        g_idx = jnp.min(jnp.where(glog == gmax, lane, LANES), axis=-1, keepdims=True)
        e_lane = lane - n_groups
        in_grp = (e_lane >= g_idx * per_group) & (e_lane < (g_idx + 1) * per_group) & (e_lane < n_exp)
        sel = jnp.where(in_grp, logits, NEG_BIG)
        m1 = jnp.max(sel, axis=-1, keepdims=True)
        i1 = jnp.min(jnp.where(sel == m1, lane, LANES), axis=-1, keepdims=True)
        sel2 = jnp.where(lane == i1, NEG_BIG, sel)
        m2 = jnp.max(sel2, axis=-1, keepdims=True)
        i2 = jnp.min(jnp.where(sel2 == m2, lane, LANES), axis=-1, keepdims=True)
        e2 = jnp.exp(m2 - m1)
        w1 = g_val / (1.0 + e2)
        w2 = g_val * e2 / (1.0 + e2)
        ids_ref[...] = jnp.where(lane == 0, i1 - n_groups, jnp.where(lane == 1, i2 - n_groups, 0))
        ew_ref[...] = jnp.where(lane == 0, w1, jnp.where(lane == 1, w2, 0.0))


def _slab_pitch(d):
    return d // LANES + SUBLANES


def _mixln(merged, w_o, h, g, b, wr_hi, wr_lo, br, alpha, n_groups, per_group):
    n, d = h.shape
    pitch = _slab_pitch(d)
    tm = _largest_tile(n, 320, 16)
    tn = _largest_tile(d, 512, LANES)
    kern = functools.partial(_mixln_kernel, alpha=alpha, tn=tn, n_groups=n_groups, per_group=per_group)
    row = lambda i, j: (i, 0)
    const = lambda i, j: (0, 0)
    return pl.pallas_call(
        kern,
        grid=(n // tm, d // tn),
        in_specs=[pl.BlockSpec((tm, d), row),
                  pl.BlockSpec((d, tn), lambda i, j: (0, j)),
                  pl.BlockSpec((tm, d), row),
                  pl.BlockSpec((1, d), const), pl.BlockSpec((1, d), const),
                  pl.BlockSpec((d, LANES), const), pl.BlockSpec((d, LANES), const), pl.BlockSpec((1, LANES), const)],
        out_specs=[pl.BlockSpec((tm * pitch, LANES), row),
                   pl.BlockSpec((tm, LANES), row), pl.BlockSpec((tm, LANES), row)],
        out_shape=[jax.ShapeDtypeStruct((n * pitch, LANES), F32),
                   jax.ShapeDtypeStruct((n, LANES), jnp.int32),
                   jax.ShapeDtypeStruct((n, LANES), F32)],
        scratch_shapes=[pltpu.VMEM((tm, d), F32)],
        compiler_params=_cparams(("parallel", "arbitrary")),
        name="mix_ln_router",
    )(merged, w_o, h, g.reshape(1, d), b.reshape(1, d), wr_hi, wr_lo, br)


def _moe_kernel(nused_ref, nvalid_ref, tok_ref, tokn_ref, dst_ref, gw_ref, h_hbm, wg_ref, wu_ref, wd_ref, out_hbm,
                xbuf, a_ref, stage, gsem, ssem, *, tm):
    i = pl.program_id(0)
    j = pl.program_id(1)
    nf = pl.num_programs(1)
    nused = nused_ref[0]
    n_chunks = xbuf.shape[0]
    pitch = stage.shape[0]

    def start_gather(idx_ref):
        def body(r, c):
            pltpu.make_async_copy(h_hbm.at[idx_ref[0, 0, r], pl.ds(0, n_chunks), :], xbuf.at[:, r, :], gsem).start()
            return c
        lax.fori_loop(0, tm, body, 0, unroll=8)

    def wait_gather():
        pltpu.make_async_copy(xbuf, xbuf, gsem).wait()

    def scatter_copy(r):
        dd = dst_ref[0, 0, r]
        return pltpu.make_async_copy(stage.at[:, r, :], out_hbm.at[dd % 2, dd // 2], ssem)

    def for_valid_rows(tile, fn):
        nv = nvalid_ref[tile]

        def eight(g, c):
            for u in range(SUBLANES):
                fn(g * SUBLANES + u)
            return c
        lax.fori_loop(0, nv // SUBLANES, eight, 0)

        def one(r, c):
            fn(r)
            return c
        lax.fori_loop((nv // SUBLANES) * SUBLANES, nv, one, 0)

    def wait_scatter(tile):
        for_valid_rows(tile, lambda r: scatter_copy(0).wait())

    @pl.when((i < nused) & (j == 0))
    def _():
        @pl.when(i == 0)
        def _():
            stage[n_chunks:pitch] = jnp.zeros((pitch - n_chunks, tm, LANES), F32)
            start_gather(tok_ref)

        wait_gather()
        a_ref[...] = jnp.concatenate([xbuf[s] for s in range(n_chunks)], axis=1).astype(BF16)

        @pl.when(i + 1 < nused)
        def _():
            start_gather(tokn_ref)

    @pl.when(i < nused)
    def _():
        a = a_ref[...]
        hg = _dot(a, wg_ref[...])
        hu = _dot(a, wu_ref[...])
        hid = (hg * _sigmoid(hg) * hu * gw_ref[...]).astype(BF16)

        @pl.when((j == 0) & (i > 0))
        def _():
            wait_scatter(i - 1)

        cw = 4 * LANES
        for c0 in range(0, wd_ref.shape[1], cw):
            yc = _dot(hid, wd_ref[:, c0:c0 + cw])

            @pl.when(j == 0)
            def _():
                for s in range(cw // LANES):
                    stage[c0 // LANES + s] = yc[:, s * LANES:(s + 1) * LANES]

            @pl.when(j > 0)
            def _():
                for s in range(cw // LANES):
                    stage[c0 // LANES + s] += yc[:, s * LANES:(s + 1) * LANES]

    @pl.when((i < nused) & (j == nf - 1))
    def _():
        for_valid_rows(i, lambda r: scatter_copy(r).start())

        @pl.when(i == nused - 1)
        def _():
            wait_scatter(i)


def _moe(h_tm, row_tok, row_dst, row_gw, tile_e, nused, tile_nv, wg, wu, wd, tm):
    n_tok, pitch, _ = h_tm.shape
    n_chunks = pitch - SUBLANES
    d = n_chunks * LANES
    n_tiles = row_tok.shape[0] // tm
    f = wg.shape[2]
    tf = _largest_tile(f, 512, LANES)
    kern = functools.partial(_moe_kernel, tm=tm)
    idx3 = row_tok.reshape(n_tiles, 1, tm)
    dst3 = row_dst.reshape(n_tiles, 1, tm)
    smem_blk = lambda im: pl.BlockSpec((1, 1, tm), im, memory_space=pltpu.SMEM)
    grid_spec = pltpu.PrefetchScalarGridSpec(
        num_scalar_prefetch=3,
        grid=(n_tiles, f // tf),
        in_specs=[smem_blk(lambda i, j, te, nu, nv: (i, 0, 0)),
                  smem_blk(lambda i, j, te, nu, nv: (jnp.minimum(i + 1, n_tiles - 1), 0, 0)),
                  smem_blk(lambda i, j, te, nu, nv: (i, 0, 0)),
                  pl.BlockSpec((tm, 1), lambda i, j, te, nu, nv: (i, 0)),
                  pl.BlockSpec(memory_space=pl.ANY),
                  pl.BlockSpec((None, d, tf), lambda i, j, te, nu, nv: (te[i], 0, j)),
                  pl.BlockSpec((None, d, tf), lambda i, j, te, nu, nv: (te[i], 0, j)),
                  pl.BlockSpec((None, tf, d), lambda i, j, te, nu, nv: (te[i], j, 0))],
        out_specs=pl.BlockSpec(memory_space=pl.ANY),
        scratch_shapes=[pltpu.VMEM((n_chunks, tm, LANES), F32), pltpu.VMEM((tm, d), BF16),
                        pltpu.VMEM((pitch, tm, LANES), F32),
                        pltpu.SemaphoreType.DMA(()), pltpu.SemaphoreType.DMA(())],
    )

    def wrapped(te_ref, nu_ref, nv_ref, *rest):
        kern(nu_ref, nv_ref, *rest)

    return pl.pallas_call(
        wrapped,
        grid_spec=grid_spec,
        out_shape=jax.ShapeDtypeStruct((2, n_tok, pitch, LANES), F32),
        compiler_params=pltpu.CompilerParams(dimension_semantics=("arbitrary", "arbitrary"),
                                             vmem_limit_bytes=MOE_VMEM_LIMIT_BYTES),
        name="moe_ffn",
    )(tile_e, nused, tile_nv, idx3, idx3, dst3, row_gw.reshape(-1, 1), h_tm, wg, wu, wd)


def _route_plan(ids, ew, n_experts, tm, n_tiles):
    n = ids.shape[0]
    ea = ids.reshape(-1)
    onehot = (ea[:, None] == jnp.arange(n_experts, dtype=jnp.int32)[None, :]).astype(jnp.int32)
    cs = jnp.cumsum(onehot, axis=0)
    rank = jnp.take_along_axis(cs, ea[:, None], axis=1)[:, 0] - 1
    cnt = cs[-1]
    pcnt = ((cnt + tm - 1) // tm) * tm
    ends = jnp.cumsum(pcnt)
    starts = ends - pcnt
    pos = starts[ea] + rank
    p_rows = n_tiles * tm
    a_idx = jnp.arange(2 * n, dtype=jnp.int32)
    row_tok = jnp.zeros((p_rows,), jnp.int32).at[pos].set(a_idx // 2)
    row_dst = jnp.zeros((p_rows,), jnp.int32).at[pos].set(a_idx)
    row_gw = jnp.zeros((p_rows,), F32).at[pos].set(ew.reshape(-1))
    nused = (ends[-1] // tm).astype(jnp.int32)
    tile_start = jnp.arange(n_tiles, dtype=jnp.int32) * tm
    tile_e = jnp.sum((tile_start[:, None] >= ends[None, :]).astype(jnp.int32), axis=1)
    tile_e = jnp.minimum(tile_e, n_experts - 1)
    tile_nv = jnp.clip(starts[tile_e] + cnt[tile_e] - tile_start, 0, tm).astype(jnp.int32)
    last_e = tile_e[jnp.maximum(nused - 1, 0)]
    in_use = jnp.arange(n_tiles) < nused
    return (row_tok, row_dst, row_gw, jnp.where(in_use, tile_e, last_e), nused.reshape(1),
            jnp.where(in_use, tile_nv, 0))


def _ln2_kernel(h_ref, m0_ref, m1_ref, g_ref, b_ref, of_ref, ob_ref, *, alpha, pitch):
    tm, d = of_ref.shape
    chunk = lambda ref, s: ref[pl.ds(s, tm, stride=pitch), :]
    x = jnp.concatenate([alpha * chunk(h_ref, s) + (chunk(m0_ref, s) + chunk(m1_ref, s)) for s in range(d // LANES)],
                        axis=1)
    y = _layernorm(x, g_ref[...], b_ref[...])
    of_ref[...] = y
    ob_ref[...] = y.astype(BF16)


def _ln2(h_tm, moe_tm, g, b, alpha):
    d = g.shape[0]
    pitch = _slab_pitch(d)
    n = h_tm.shape[0] // pitch
    tm = _largest_tile(n, 160, 16)
    return pl.pallas_call(
        functools.partial(_ln2_kernel, alpha=alpha, pitch=pitch),
        grid=(n // tm,),
        in_specs=[pl.BlockSpec((tm * pitch, LANES), lambda i: (i, 0)),
                  pl.BlockSpec((None, tm * pitch, LANES), lambda i: (0, i, 0)),
                  pl.BlockSpec((None, tm * pitch, LANES), lambda i: (1, i, 0)),
                  pl.BlockSpec((1, d), lambda i: (0, 0)), pl.BlockSpec((1, d), lambda i: (0, 0))],
        out_specs=[pl.BlockSpec((tm, d), lambda i: (i, 0)), pl.BlockSpec((tm, d), lambda i: (i, 0))],
        out_shape=[jax.ShapeDtypeStruct((n, d), F32), jax.ShapeDtypeStruct((n, d), BF16)],
        compiler_params=_cparams(("parallel",)),
        name="ln2",
    )(h_tm, moe_tm, moe_tm, g.reshape(1, d), b.reshape(1, d))


def kernel(x_prompt, x_sample, cache_k, cache_v, page_table, state_ssm, state_conv, meta_tokens, ln_emb_g, ln_emb_b,
           w_in, conv_w, conv_b, dt_bias, a_log, d_skip, ssd_norm_g, w_ssd_out, w_sb_out, sb_bias, w_o, ln1_g, ln1_b,
           wr_group, br_group, wr_expert, br_expert, w_gate, w_up, w_down, ln2_g, ln2_b):
    n_b, seq, d = x_prompt.shape
    n_db, dseq, _ = x_sample.shape
    depth = w_in.shape[0]
    c_dim = conv_w.shape[-1]
    n_ssd_heads = dt_bias.shape[-1]
    p_dim = d // n_ssd_heads
    nst = (c_dim - d) // (2 * N_SSD_GROUPS)
    sb_heads = sb_bias.shape[-1]
    hd = cache_k.shape[-1]
    sbw = sb_heads * hd
    page = cache_k.shape[2]
    n_exp = w_gate.shape[1]
    per_group = n_exp // N_ROUTE_GROUPS
    alpha = float((2 * depth) ** 0.25)
    n_real = n_b * seq
    n_meta = n_b * N_META
    n_samp = n_db * dseq
    n_tok = n_real + n_meta + n_samp
    assert seq % SSD_CHUNK == 0 and n_ssd_heads <= LANES and N_ROUTE_GROUPS + n_exp <= LANES and d % 512 == 0

    sizes = (d, c_dim, n_ssd_heads, sbw, sbw, sbw, d, d)
    offs = [0]
    for s in sizes:
        offs.append(offs[-1] + s)
    o_z, o_xbc, o_dt, o_q, o_k, o_v, o_gs, o_gb = offs[:8]

    moe_tm = 512 if n_tok >= 4096 else 64
    n_tiles = (2 * n_tok + n_exp * (moe_tm - 1)) // moe_tm + 1

    x_all = jnp.concatenate([x_prompt.reshape(n_real, d),
                             jnp.broadcast_to(meta_tokens[None].astype(x_prompt.dtype), (n_b, N_META, d)).reshape(n_meta, d),
                             x_sample.reshape(n_samp, d)], axis=0)
    h, h_bf = _embed_ln(x_all, ln_emb_g, ln_emb_b)

    lane_pad = lambda v: jnp.pad(v.astype(F32), (0, LANES - v.shape[0])).reshape(1, LANES)
    eexp = (jnp.arange(LANES)[:, None] == (jnp.arange(d)[None, :] // p_dim)).astype(BF16)
    cache_k4 = cache_k.reshape(cache_k.shape[0], depth, page * sb_heads, hd)
    cache_v4 = cache_v.reshape(cache_v.shape[0], depth, page * sb_heads, hd)

    outs = {k: [] for k in ("sp", "cp", "ss", "cs")}
    kv_all = jnp.zeros((depth, n_tok, 2 * sbw), F32)
    for l in range(depth):
        wl = w_in[l]
        seg = lambda c0, width: wl[:, c0:c0 + width].astype(BF16)
        w_dt = jnp.pad(wl[:, o_dt:o_dt + n_ssd_heads], ((0, 0), (0, LANES - n_ssd_heads))).astype(BF16)

        u_z = _matmul(h_bf, seg(o_z, d), BF16, "in_proj_z")
        u_q = _matmul(h_bf, seg(o_q, sbw), BF16, "in_proj_q")
        u_g = _matmul(h_bf, seg(o_gs, 2 * d), BF16, "in_proj_gates")
        u_xbc = _matmul(h_bf, seg(o_xbc, c_dim), F32, "in_proj_xbc")
        u_dt = _matmul(h_bf, w_dt, F32, "in_proj_dt")
        kv_all = _matmul_into_layer(h_bf, seg(o_k, 2 * sbw), kv_all, l, "in_proj_kv")

        consts = (jnp.pad(conv_w[l], ((0, SUBLANES - conv_w.shape[1]), (0, 0))), conv_b[l].reshape(1, c_dim),
                  lane_pad(dt_bias[l]), lane_pad(-jnp.exp(a_log[l].astype(F32))),
                  jnp.repeat(d_skip[l], p_dim).reshape(1, d), ssd_norm_g[l].reshape(1, d), eexp)
        xm = u_xbc[n_real:n_real + n_meta]
        y_meta, st_meta = _ssd_small(xm, jnp.zeros((n_b, SUBLANES, c_dim), F32),
                                     u_z[n_real:n_real + n_meta].astype(F32), u_dt[n_real:n_real + n_meta],
                                     jnp.zeros((n_b, nst, d), F32), consts, n_b, N_META, d, n_ssd_heads, nst, "ssd_meta")
        y_real, st_real = _ssd_main(u_xbc, u_z, u_dt, st_meta, consts, n_b, seq, d, n_ssd_heads, nst)
        xs_ = u_xbc[n_real + n_meta:]
        prev_s = jnp.pad(state_conv[l], ((0, 0), (SUBLANES - state_conv.shape[2], 0), (0, 0)))
        st0_s = jnp.transpose(state_ssm[l].astype(F32), (0, 3, 1, 2)).reshape(n_db, nst, d)
        y_samp, st_samp = _ssd_small(xs_, prev_s, u_z[n_real + n_meta:].astype(F32), u_dt[n_real + n_meta:],
                                     st0_s, consts, n_db, dseq, d, n_ssd_heads, nst, "ssd_sample")
        y_ssd = jnp.concatenate([y_real, y_meta.astype(BF16), y_samp.astype(BF16)], axis=0)

        o_real, o_meta = _sb_prompt(sb_bias[l].astype(F32), u_q, kv_all, l, n_b, seq, sb_heads, hd)
        q_s = u_q[n_real + n_meta:].astype(F32)
        kv_s = kv_all[l, n_real + n_meta:]
        o_samp = _sb_sample(page_table, q_s, kv_s[:, :sbw], kv_s[:, sbw:],
                            jnp.repeat(sb_bias[l].astype(F32), dseq).reshape(1, sb_heads * dseq),
                            cache_k4, cache_v4, l, dseq)
        o_sb = jnp.concatenate([o_real, o_meta, o_samp.astype(BF16)], axis=0)

        merged = _merge(y_ssd, o_sb, w_ssd_out[l].astype(BF16), w_sb_out[l].astype(BF16), u_g, 0, d)
        wr = jnp.concatenate([wr_group[l], jnp.transpose(wr_expert[l], (1, 0, 2)).reshape(d, n_exp)], axis=1)
        wr = jnp.pad(wr.astype(F32), ((0, 0), (0, LANES - wr.shape[1])))
        wr_hi = wr.astype(BF16)
        wr_lo = (wr - wr_hi.astype(F32)).astype(BF16)
        br = lane_pad(jnp.concatenate([br_group[l], br_expert[l].reshape(-1)]))
        h1_tm, ids, ew = _mixln(merged, w_o[l].astype(BF16), h, ln1_g[l], ln1_b[l], wr_hi, wr_lo, br, alpha,
                                N_ROUTE_GROUPS, per_group)
        plan = _route_plan(ids[:, :2], ew[:, :2], n_exp, moe_tm, n_tiles)
        pitch = _slab_pitch(d)
        moe_tm_out = _moe(h1_tm.reshape(n_tok, pitch, LANES), *plan, w_gate[l].astype(BF16), w_up[l].astype(BF16),
                          w_down[l].astype(BF16), moe_tm)
        h, h_bf = _ln2(h1_tm, moe_tm_out.reshape(2, n_tok * pitch, LANES), ln2_g[l], ln2_b[l], alpha)

        to_state = lambda st, nb: jnp.transpose(st.reshape(nb, nst, n_ssd_heads, p_dim), (0, 2, 3, 1))
        outs["sp"].append(to_state(st_real, n_b)); outs["ss"].append(to_state(st_samp, n_db))
        outs["cp"].append(jnp.stack([u_xbc[(b + 1) * seq - 3:(b + 1) * seq] for b in range(n_b)]))
        outs["cs"].append(xs_.reshape(n_db, dseq, c_dim)[:, dseq - 3:, :])

    def kv_out(col0):
        real = kv_all[:, :n_real, col0:col0 + sbw].reshape(depth, n_b, seq, sb_heads, hd)
        meta = kv_all[:, n_real:n_real + n_meta, col0:col0 + sbw].reshape(depth, n_b, N_META, sb_heads, hd)
        samp = kv_all[:, n_real + n_meta:, col0:col0 + sbw].reshape(depth, n_db, dseq, sb_heads, hd)
        return jnp.concatenate([meta, real], axis=2), samp
    kp, ks = kv_out(0)
    vp, vs = kv_out(sbw)
    st = lambda k: jnp.stack(outs[k])
    return (h[:n_real].reshape(n_b, seq, d), h[n_real + n_meta:].reshape(n_db, dseq, d),
            kp, vp, st("sp"), st("cp"), ks, vs, st("ss"), st("cs"))
```

```python
import functools
import math

import jax
import jax.numpy as jnp
from jax import lax
from jax.experimental import pallas as pl
from jax.experimental.pallas import tpu as pltpu

F32 = jnp.float32
BF16 = jnp.bfloat16

N_META = 16
SSD_CHUNK = 128
N_SSD_GROUPS = 8
N_ROUTE_GROUPS = 4
LN_EPS = 1e-5
RMS_EPS = 1e-5
NEG_BIG = -1e30

LANES = 128
SUBLANES = 8
VMEM_LIMIT_BYTES = 56 * 1024 * 1024
MOE_VMEM_LIMIT_BYTES = 60 * 1024 * 1024


def _cparams(sem):
    return pltpu.CompilerParams(dimension_semantics=sem, vmem_limit_bytes=VMEM_LIMIT_BYTES)


def _largest_tile(n, cap, mult):
    best = None
    for t in range(mult, min(n, cap) + 1, mult):
        if n % t == 0:
            best = t
    assert best is not None, (n, cap, mult)
    return best


def _dot(a, b):
    return jnp.dot(a, b, preferred_element_type=F32)


def _dot_nt(a, b):
    return lax.dot_general(a, b, (((1,), (1,)), ((), ())), preferred_element_type=F32)


def _dot_tn(a, b):
    return lax.dot_general(a, b, (((0,), (0,)), ((), ())), preferred_element_type=F32)


def _split3(x):
    hi = x.astype(BF16)
    r = x - hi.astype(F32)
    mid = r.astype(BF16)
    lo = (r - mid.astype(F32)).astype(BF16)
    return hi, mid, lo


def _split2(x):
    hi = x.astype(BF16)
    lo = (x - hi.astype(F32)).astype(BF16)
    return hi, lo


def _softplus(x):
    return jnp.maximum(x, 0.0) + jnp.log(1.0 + jnp.exp(-jnp.abs(x)))


def _sigmoid(x):
    return 1.0 / (1.0 + jnp.exp(-x))


def _layernorm(x, g, b):
    mu = jnp.mean(x, axis=-1, keepdims=True)
    xc = x - mu
    var = jnp.mean(xc * xc, axis=-1, keepdims=True)
    return xc * lax.rsqrt(var + LN_EPS) * g + b


def _ln_kernel(x_ref, g_ref, b_ref, of_ref, ob_ref):
    y = _layernorm(x_ref[...], g_ref[...], b_ref[...])
    of_ref[...] = y
    ob_ref[...] = y.astype(BF16)


def _embed_ln(x, g, b):
    n, d = x.shape
    tm = _largest_tile(n, 320, 16)
    return pl.pallas_call(
        _ln_kernel,
        grid=(n // tm,),
        in_specs=[pl.BlockSpec((tm, d), lambda i: (i, 0)),
                  pl.BlockSpec((1, d), lambda i: (0, 0)),
                  pl.BlockSpec((1, d), lambda i: (0, 0))],
        out_specs=[pl.BlockSpec((tm, d), lambda i: (i, 0)),
                   pl.BlockSpec((tm, d), lambda i: (i, 0))],
        out_shape=[jax.ShapeDtypeStruct((n, d), F32), jax.ShapeDtypeStruct((n, d), BF16)],
        compiler_params=_cparams(("parallel",)),
        name="embed_ln",
    )(x, g.reshape(1, d), b.reshape(1, d))


def _mm_kernel(x_ref, w_ref, o_ref):
    o_ref[...] = _dot(x_ref[...], w_ref[...]).astype(o_ref.dtype)


def _matmul(x, w, out_dtype, name):
    n, k = x.shape
    m = w.shape[1]
    tm = _largest_tile(n, 640, 16)
    tn = _largest_tile(m, 1024, LANES)
    return pl.pallas_call(
        _mm_kernel,
        grid=(n // tm, m // tn),
        in_specs=[pl.BlockSpec((tm, k), lambda i, j: (i, 0)),
                  pl.BlockSpec((k, tn), lambda i, j: (0, j))],
        out_specs=pl.BlockSpec((tm, tn), lambda i, j: (i, j)),
        out_shape=jax.ShapeDtypeStruct((n, m), out_dtype),
        compiler_params=_cparams(("parallel", "parallel")),
        name=name,
    )(x, w)


def _mm_into_kernel(x_ref, w_ref, buf_ref, o_ref):
    del buf_ref
    o_ref[...] = _dot(x_ref[...], w_ref[...]).astype(o_ref.dtype)


def _matmul_into_layer(x, w, buf, layer, name):
    n, k = x.shape
    m = w.shape[1]
    tm = _largest_tile(n, 640, 16)
    tn = _largest_tile(m, 1024, LANES)
    return pl.pallas_call(
        _mm_into_kernel,
        grid=(n // tm, m // tn),
        in_specs=[pl.BlockSpec((tm, k), lambda i, j: (i, 0)), pl.BlockSpec((k, tn), lambda i, j: (0, j)),
                  pl.BlockSpec(memory_space=pl.ANY)],
        out_specs=pl.BlockSpec((None, tm, tn), lambda i, j: (layer, i, j)),
        out_shape=jax.ShapeDtypeStruct(buf.shape, buf.dtype),
        input_output_aliases={2: 0},
        compiler_params=_cparams(("parallel", "parallel")),
        name=name,
    )(x, w, buf)


def _ssd_chunk(xpad_ref, act_ref, st_ref, z, dtraw, n_valid, cw_ref, cb_ref, dtb_ref, aneg_ref, dskip_ref,
               ng_ref, eexp_ref, y_write, d_model, n_heads):
    q = SSD_CHUNK
    c_dim = xpad_ref.shape[1]
    nst = st_ref.shape[0]
    g_cnt = N_SSD_GROUPS
    gw = d_model // g_cnt
    heads_per_group = n_heads // g_cnt
    b_off = d_model
    c_off = d_model + g_cnt * nst

    ct = 512
    for c0 in range(0, c_dim, ct):
        acc = cb_ref[:, c0:c0 + ct]
        for i in range(4):
            acc = acc + xpad_ref[pl.ds(SUBLANES - 3 + i, q), c0:c0 + ct] * cw_ref[i:i + 1, c0:c0 + ct]
        act_ref[:, c0:c0 + ct] = acc * _sigmoid(acc)

    rows = lax.broadcasted_iota(jnp.int32, (q, q), 0)
    cols = lax.broadcasted_iota(jnp.int32, (q, q), 1)
    tri = cols <= rows
    tril_ones = jnp.where(tri, 1.0, 0.0).astype(BF16)
    low_half = cols < (LANES // 2)

    dt = _softplus(dtraw + dtb_ref[...])
    if n_valid < q:
        dt = jnp.where(rows < n_valid, dt, 0.0)
    da = dt * aneg_ref[...]
    d_hi, d_mid, d_lo = _split3(da)
    acum = _dot(tril_ones, d_hi) + _dot(tril_ones, d_mid) + _dot(tril_ones, d_lo)
    last = acum[q - 1:q, :]
    w_end = jnp.exp(last - acum) * dt
    acum_t = acum.T
    dt_t = dt.T
    eexp = eexp_ref[...]
    w_exp = _dot(w_end.astype(BF16), eexp)
    cd = jnp.broadcast_to(jnp.exp(last), (SUBLANES, LANES))
    cd_hi, cd_lo = _split2(cd)
    cd_exp = (_dot(cd_hi, eexp) + _dot(cd_lo, eexp))[0:1, :]

    for g in range(g_cnt):
        bm = act_ref[:, b_off + g * nst:b_off + (g + 1) * nst]
        cm = act_ref[:, c_off + g * nst:c_off + (g + 1) * nst]
        bm_b = bm.astype(BF16)
        cb_mat = _dot_nt(cm.astype(BF16), bm_b)
        ys = []
        for pr in range(heads_per_group // 2):
            col0 = g * gw + pr * LANES
            lhs = []
            for hh in range(2):
                h = g * heads_per_group + 2 * pr + hh
                colb = jnp.broadcast_to(acum[:, h:h + 1], (q, q))
                seg = colb - acum_t[h:h + 1, :]
                dec = jnp.exp(jnp.where(tri, seg, NEG_BIG))
                lhs.append((cb_mat * dec * dt_t[h:h + 1, :]).astype(BF16))
                lhs.append((cm * jnp.exp(colb)).astype(BF16))
            xp = act_ref[:, col0:col0 + LANES]
            sp = st_ref[:, col0:col0 + LANES]
            rhs = jnp.concatenate([jnp.where(low_half, xp, 0.0), jnp.where(low_half, sp, 0.0),
                                   jnp.where(low_half, 0.0, xp), jnp.where(low_half, 0.0, sp)], axis=0).astype(BF16)
            yp = _dot(jnp.concatenate(lhs, axis=1), rhs)
            yp = yp + dskip_ref[:, col0:col0 + LANES] * xp
            zp = z[:, col0:col0 + LANES]
            ys.append(yp * (zp * _sigmoid(zp)))
        ss = ys[0] * ys[0]
        for yp in ys[1:]:
            ss = ss + yp * yp
        inv = lax.rsqrt(jnp.sum(ss, axis=-1, keepdims=True) / gw + RMS_EPS)
        for pr, yp in enumerate(ys):
            col0 = g * gw + pr * LANES
            y_write(col0, yp * inv * ng_ref[:, col0:col0 + LANES])
        xg = act_ref[:, g * gw:(g + 1) * gw]
        xw = (xg * w_exp[:, g * gw:(g + 1) * gw]).astype(BF16)
        dst = _dot_tn(bm_b, xw)
        st_ref[:, g * gw:(g + 1) * gw] = st_ref[:, g * gw:(g + 1) * gw] * cd_exp[:, g * gw:(g + 1) * gw] + dst


def _ssd_main_kernel(xbc_ref, xmeta_ref, z_ref, dt_ref, st0_ref, cw_ref, cb_ref, dtb_ref, aneg_ref, dskip_ref,
                     ng_ref, eexp_ref, y_ref, stout_ref, xpad_ref, act_ref, st_ref, *, d_model, n_heads):
    c = pl.program_id(1)

    @pl.when(c == 0)
    def _():
        st_ref[...] = st0_ref[0]
        xpad_ref[0:SUBLANES, :] = xmeta_ref[N_META - SUBLANES:N_META, :]

    xpad_ref[SUBLANES:SUBLANES + SSD_CHUNK, :] = xbc_ref[...]

    def y_write(col0, val):
        y_ref[:, col0:col0 + val.shape[1]] = val.astype(y_ref.dtype)

    _ssd_chunk(xpad_ref, act_ref, st_ref, z_ref[...].astype(F32), dt_ref[...], SSD_CHUNK, cw_ref, cb_ref, dtb_ref,
               aneg_ref, dskip_ref, ng_ref, eexp_ref, y_write, d_model, n_heads)
    xpad_ref[0:SUBLANES, :] = xbc_ref[SSD_CHUNK - SUBLANES:SSD_CHUNK, :]

    @pl.when(c == pl.num_programs(1) - 1)
    def _():
        stout_ref[0] = st_ref[...]


def _ssd_small_kernel(x_ref, prev_ref, z_ref, dt_ref, st0_ref, cw_ref, cb_ref, dtb_ref, aneg_ref, dskip_ref,
                      ng_ref, eexp_ref, y_ref, stout_ref, xpad_ref, act_ref, st_ref, *, d_model, n_heads, seq):
    st_ref[...] = st0_ref[0]
    xpad_ref[0:SUBLANES, :] = prev_ref[0]
    xpad_ref[SUBLANES:SUBLANES + seq, :] = x_ref[...]
    xpad_ref[SUBLANES + seq:, :] = jnp.zeros((SSD_CHUNK - seq, xpad_ref.shape[1]), F32)
    pad = jnp.zeros((SSD_CHUNK - seq, d_model), F32)
    z = jnp.concatenate([z_ref[...], pad], axis=0)
    dtr = jnp.concatenate([dt_ref[...], jnp.zeros((SSD_CHUNK - seq, LANES), F32)], axis=0)

    def y_write(col0, val):
        y_ref[:, col0:col0 + val.shape[1]] = val[0:seq, :]

    _ssd_chunk(xpad_ref, act_ref, st_ref, z, dtr, seq, cw_ref, cb_ref, dtb_ref, aneg_ref, dskip_ref, ng_ref,
               eexp_ref, y_write, d_model, n_heads)
    stout_ref[0] = st_ref[...]


def _ssd_consts_specs(c_dim, d_model, idx):
    return [pl.BlockSpec((SUBLANES, c_dim), idx), pl.BlockSpec((1, c_dim), idx), pl.BlockSpec((1, LANES), idx),
            pl.BlockSpec((1, LANES), idx), pl.BlockSpec((1, d_model), idx), pl.BlockSpec((1, d_model), idx),
            pl.BlockSpec((LANES, d_model), idx)]


def _ssd_main(xbc, z_src, dt, st0, consts, n_batch, seq, d_model, n_heads, nst):
    c_dim = xbc.shape[1]
    nc = seq // SSD_CHUNK
    meta_blk0 = n_batch * seq // N_META
    kern = functools.partial(_ssd_main_kernel, d_model=d_model, n_heads=n_heads)
    return pl.pallas_call(
        kern,
        grid=(n_batch, nc),
        in_specs=[pl.BlockSpec((SSD_CHUNK, c_dim), lambda b, c: (b * nc + c, 0)),
                  pl.BlockSpec((N_META, c_dim), lambda b, c: (meta_blk0 + b, 0)),
                  pl.BlockSpec((SSD_CHUNK, d_model), lambda b, c: (b * nc + c, 0)),
                  pl.BlockSpec((SSD_CHUNK, LANES), lambda b, c: (b * nc + c, 0)),
                  pl.BlockSpec((1, nst, d_model), lambda b, c: (b, 0, 0))]
        + _ssd_consts_specs(c_dim, d_model, lambda b, c: (0, 0)),
        out_specs=[pl.BlockSpec((SSD_CHUNK, d_model), lambda b, c: (b * nc + c, 0)),
                   pl.BlockSpec((1, nst, d_model), lambda b, c: (b, 0, 0))],
        out_shape=[jax.ShapeDtypeStruct((n_batch * seq, d_model), BF16),
                   jax.ShapeDtypeStruct((n_batch, nst, d_model), F32)],
        scratch_shapes=[pltpu.VMEM((SUBLANES + SSD_CHUNK, c_dim), F32), pltpu.VMEM((SSD_CHUNK, c_dim), F32),
                        pltpu.VMEM((nst, d_model), F32)],
        compiler_params=_cparams(("parallel", "arbitrary")),
        name="ssd_main",
    )(xbc, xbc, z_src, dt, st0, *consts)


def _ssd_small(x, prev, z, dt, st0, consts, n_seq, seq, d_model, n_heads, nst, name):
    c_dim = x.shape[1]
    kern = functools.partial(_ssd_small_kernel, d_model=d_model, n_heads=n_heads, seq=seq)
    return pl.pallas_call(
        kern,
        grid=(n_seq,),
        in_specs=[pl.BlockSpec((seq, c_dim), lambda i: (i, 0)),
                  pl.BlockSpec((1, SUBLANES, c_dim), lambda i: (i, 0, 0)),
                  pl.BlockSpec((seq, d_model), lambda i: (i, 0)),
                  pl.BlockSpec((seq, LANES), lambda i: (i, 0)),
                  pl.BlockSpec((1, nst, d_model), lambda i: (i, 0, 0))]
        + _ssd_consts_specs(c_dim, d_model, lambda i: (0, 0)),
        out_specs=[pl.BlockSpec((seq, d_model), lambda i: (i, 0)),
                   pl.BlockSpec((1, nst, d_model), lambda i: (i, 0, 0))],
        out_shape=[jax.ShapeDtypeStruct((n_seq * seq, d_model), F32),
                   jax.ShapeDtypeStruct((n_seq, nst, d_model), F32)],
        scratch_shapes=[pltpu.VMEM((SUBLANES + SSD_CHUNK, c_dim), F32), pltpu.VMEM((SSD_CHUNK, c_dim), F32),
                        pltpu.VMEM((nst, d_model), F32)],
        compiler_params=_cparams(("parallel",)),
        name=name,
    )(x, prev, z, dt, st0, *consts)


SB_CLAMP = 80.0
SB_PAGES_PER_STEP = 4
SB_KEY_TILE = 256
SB_QUERY_TILE = 512


def _suffix_scan_matrix(nk):
    r = lax.broadcasted_iota(jnp.int32, (nk + SUBLANES, nk), 0)
    c = lax.broadcasted_iota(jnp.int32, (nk + SUBLANES, nk), 1)
    return jnp.where((c > r) | (r >= nk), 1.0, 0.0).astype(BF16)


def _sb_tile(s_t, bias, scale, visible, run, scan_mat):
    return _sb_weights(_sb_scan(_sb_logits(s_t, bias, scale, visible), scan_mat), run)


def _sb_logits(s_t, bias, scale, visible):
    z = s_t * scale + bias
    t = jnp.minimum(z, SB_CLAMP)
    lg = jnp.log(1.0 + jnp.exp(t))
    lneg = (t - z) - lg
    vis = None
    if visible is not None:
        vis = visible(lax.broadcasted_iota(jnp.int32, s_t.shape, 0), lax.broadcasted_iota(jnp.int32, s_t.shape, 1))
        lneg = jnp.where(vis, lneg, 0.0)
    return (t - lg, vis) + _split2(lneg)


def _sb_scan(logits, scan_mat):
    logsig, vis, l_hi, l_lo = logits
    return logsig, vis, _dot(scan_mat, l_hi) + _dot(scan_mat, l_lo)


def _sb_weights(scanned, run):
    logsig, vis, sc = scanned
    nk = logsig.shape[0]
    w = jnp.exp(logsig + (sc[0:nk, :] + run))
    if vis is not None:
        w = jnp.where(vis, w, 0.0)
    return w, run + sc[nk:nk + 1, :]


def _sb_prompt_kernel(bias_ref, q_ref, qm_ref, k_ref, km_ref, v_ref, vm_ref, o_ref, om_ref, kb_ref, vt_ref,
                      *, seq, tq, tk, scale):
    h = pl.program_id(1)
    bias = bias_ref[h]
    n_t = seq // tq
    n_diag = tq // tk
    hd = k_ref.shape[1]
    for t in range(seq // tk):
        kb_ref[t * tk:(t + 1) * tk, :] = k_ref[t * tk:(t + 1) * tk, :].astype(BF16)
        vt_ref[:, t * tk:(t + 1) * tk] = v_ref[t * tk:(t + 1) * tk, :].T.astype(BF16)
    zpad = jnp.zeros((LANES - N_META, hd), F32)
    km = jnp.concatenate([km_ref[...], zpad], axis=0).astype(BF16)
    vm_t = jnp.concatenate([vm_ref[...], zpad], axis=0).T.astype(BF16)
    scan_t = _suffix_scan_matrix(tk)
    scan_m = _suffix_scan_matrix(LANES)

    def diag_vis(off):
        return lambda krow, qcol: krow + off < qcol

    def meta_vis(krow, qcol):
        return krow < N_META

    def q_tile(i, carry):
        q0 = pl.multiple_of(i * tq, tq)
        qt = q_ref[pl.ds(q0, tq), :]

        def k_tiles(k0s, visibles, c2):
            o_acc, run2 = c2
            logits = [_sb_logits(_dot_nt(kb_ref[pl.ds(k0, tk), :], qt), bias, scale, vis)
                      for k0, vis in zip(k0s, visibles)]
            scanned = [_sb_scan(lg, scan_t) for lg in logits]
            for k0, sc in zip(k0s, scanned):
                w2, run2 = _sb_weights(sc, run2)
                o_acc = o_acc + _dot(vt_ref[:, pl.ds(k0, tk)], w2.astype(BF16))
            return o_acc, run2

        sc_m = _sb_scan(_sb_logits(_dot_nt(km, qt), bias, scale, meta_vis), scan_m)
        offs = [u * tk for u in reversed(range(n_diag))]
        o_t, run = k_tiles([pl.multiple_of(q0 + off, tk) for off in offs], [diag_vis(off) for off in offs],
                           (jnp.zeros((hd, tq), F32), jnp.zeros((1, tq), F32)))
        o_t, run = lax.fori_loop(
            0, i,
            lambda pp, c2: k_tiles([pl.multiple_of((i - 1 - pp) * tq + u * tk, tk) for u in reversed(range(n_diag))],
                                   [None] * n_diag, c2),
            (o_t, run))
        w3, run = _sb_weights(sc_m, run)
        o_t = o_t + _dot(vm_t, w3.astype(BF16))
        o_ref[pl.ds(q0, tq), :] = o_t.T.astype(o_ref.dtype)
        return carry

    lax.fori_loop(0, n_t, q_tile, 0)

    qm = jnp.concatenate([qm_ref[...], jnp.zeros((LANES - N_META, hd), qm_ref.dtype)], axis=0)
    s_m = _dot_nt(km, qm)
    w_m, _ = _sb_tile(s_m, bias, scale, diag_vis(0), jnp.zeros((1, LANES), F32), scan_m)
    o_m = _dot(vm_t, w_m.astype(BF16)).T
    om_ref[...] = o_m[0:N_META, :].astype(om_ref.dtype)


def _sb_prompt(sb_bias, q_src, kv_all, layer, n_batch, seq, n_heads, hd):
    tk = _largest_tile(seq, SB_KEY_TILE, LANES)
    tq = _largest_tile(seq, SB_QUERY_TILE, tk)
    mb0 = n_batch * seq // N_META
    kern = functools.partial(_sb_prompt_kernel, seq=seq, tq=tq, tk=tk, scale=hd ** -0.5)
    return pl.pallas_call(
        kern,
        grid=(n_batch, n_heads),
        in_specs=[pl.BlockSpec(memory_space=pltpu.SMEM),
                  pl.BlockSpec((seq, hd), lambda b, h: (b, h)),
                  pl.BlockSpec((N_META, hd), lambda b, h: (mb0 + b, h)),
                  pl.BlockSpec((None, seq, hd), lambda b, h: (layer, b, h)),
                  pl.BlockSpec((None, N_META, hd), lambda b, h: (layer, mb0 + b, h)),
                  pl.BlockSpec((None, seq, hd), lambda b, h: (layer, b, n_heads + h)),
                  pl.BlockSpec((None, N_META, hd), lambda b, h: (layer, mb0 + b, n_heads + h))],
        out_specs=[pl.BlockSpec((seq, hd), lambda b, h: (b, h)),
                   pl.BlockSpec((N_META, hd), lambda b, h: (b, h))],
        out_shape=[jax.ShapeDtypeStruct((n_batch * seq, n_heads * hd), BF16),
                   jax.ShapeDtypeStruct((n_batch * N_META, n_heads * hd), BF16)],
        scratch_shapes=[pltpu.VMEM((seq, hd), BF16), pltpu.VMEM((hd, seq), BF16)],
        compiler_params=_cparams(("parallel", "parallel")),
        name="sb_prompt",
    )(sb_bias, q_src, q_src, kv_all, kv_all, kv_all, kv_all)


def _sb_sample_kernel(pt_ref, q_ref, kn_ref, vn_ref, bias_ref, *refs, n_heads, hd, dseq, page, scale):
    cache_refs, (o_ref, qbd_ref, acc_ref, run_ref) = refs[:-4], refs[-4:]
    p = pl.program_id(1)
    width = n_heads * hd
    nq = n_heads * dseq
    bias = bias_ref[...]
    scan_p = _suffix_scan_matrix(page)

    def new_vis(krow, lane):
        return krow < lane % dseq

    def attend(k_blk, v_blk, visible, run):
        s_t = _dot_nt(k_blk, qbd_ref[...])
        w, run = _sb_tile(s_t, bias, scale, visible, run, scan_p)
        return _dot(w.T.astype(BF16), v_blk), run

    @pl.when(p == 0)
    def _():
        q_t = jnp.concatenate([q_ref[...]] * n_heads, axis=0)
        r = lax.broadcasted_iota(jnp.int32, (nq, width), 0) // dseq
        c = lax.broadcasted_iota(jnp.int32, (nq, width), 1) // hd
        qbd_ref[...] = jnp.where(r == c, q_t, 0.0).astype(BF16)
        zpad = jnp.zeros((page - dseq, width), F32)
        k_new = jnp.concatenate([kn_ref[...], zpad], axis=0).astype(BF16)
        v_new = jnp.concatenate([vn_ref[...], zpad], axis=0).astype(BF16)
        acc_ref[...], run_ref[...] = attend(k_new, v_new, new_vis, jnp.zeros(run_ref.shape, F32))

    def page_rows(ref):
        return jnp.concatenate([ref[pl.ds(hh, page, stride=n_heads), :] for hh in range(n_heads)],
                               axis=1).astype(BF16)

    n_pg = len(cache_refs) // 2
    qbd = qbd_ref[...]
    logits = [_sb_logits(_dot_nt(page_rows(kc_ref), qbd), bias, scale, None) for kc_ref in cache_refs[:n_pg]]
    scanned = [_sb_scan(lg, scan_p) for lg in logits]
    run = run_ref[...]
    contrib = None
    for sc, vc_ref in zip(scanned, cache_refs[n_pg:]):
        w, run = _sb_weights(sc, run)
        c = _dot(w.T.astype(BF16), page_rows(vc_ref))
        contrib = c if contrib is None else contrib + c
    acc_ref[...] += contrib
    run_ref[...] = run

    @pl.when(p == pl.num_programs(1) - 1)
    def _():
        o_ref[...] = jnp.concatenate(
            [acc_ref[hh * dseq:(hh + 1) * dseq, hh * hd:(hh + 1) * hd] for hh in range(n_heads)], axis=1)


def _sb_sample(page_table, q, k_new, v_new, bias_lane, cache_k, cache_v, layer, dseq):
    n_db, n_pages = page_table.shape
    hd = cache_k.shape[-1]
    width = q.shape[1]
    n_heads = width // hd
    page = cache_k.shape[2] // n_heads
    nq = n_heads * dseq
    assert nq <= LANES and dseq == SUBLANES, (n_heads, dseq)
    kern = functools.partial(_sb_sample_kernel, n_heads=n_heads, hd=hd, dseq=dseq, page=page, scale=hd ** -0.5)
    pps = _largest_tile(n_pages, SB_PAGES_PER_STEP, 1)

    def cache_spec(k):
        return pl.BlockSpec((None, None, page * n_heads, hd),
                            lambda b, p, pt: (pt[b, n_pages - 1 - (p * pps + k)], layer, 0, 0))

    grid_spec = pltpu.PrefetchScalarGridSpec(
        num_scalar_prefetch=1,
        grid=(n_db, n_pages // pps),
        in_specs=[pl.BlockSpec((dseq, width), lambda b, p, pt: (b, 0)),
                  pl.BlockSpec((dseq, width), lambda b, p, pt: (b, 0)),
                  pl.BlockSpec((dseq, width), lambda b, p, pt: (b, 0)),
                  pl.BlockSpec((1, nq), lambda b, p, pt: (0, 0))]
        + [cache_spec(k) for k in range(pps)] * 2,
        out_specs=pl.BlockSpec((dseq, width), lambda b, p, pt: (b, 0)),
        scratch_shapes=[pltpu.VMEM((nq, width), BF16), pltpu.VMEM((nq, width), F32), pltpu.VMEM((1, nq), F32)],
    )
    return pl.pallas_call(
        kern,
        grid_spec=grid_spec,
        out_shape=jax.ShapeDtypeStruct((n_db * dseq, width), F32),
        compiler_params=_cparams(("parallel", "arbitrary")),
        name="sb_sample",
    )(page_table, q, k_new, v_new, bias_lane, *([cache_k] * pps + [cache_v] * pps))


def _merge_kernel(y_ref, o_ref, w1_ref, w2_ref, g1_ref, g2_ref, out_ref):
    a = _dot(y_ref[...], w1_ref[...])
    b = _dot(o_ref[...], w2_ref[...])
    out_ref[...] = (_sigmoid(g1_ref[...].astype(F32)) * a + _sigmoid(g2_ref[...].astype(F32)) * b).astype(BF16)


def _merge(y, o, w1, w2, g_src, g1_col0, g2_col0):
    n, d1 = y.shape
    d2 = o.shape[1]
    m = w1.shape[1]
    tm = _largest_tile(n, 640, 16)
    tn = _largest_tile(math.gcd(math.gcd(m, g1_col0), g2_col0), 512, LANES)
    b1, b2 = g1_col0 // tn, g2_col0 // tn
    return pl.pallas_call(
        _merge_kernel,
        grid=(n // tm, m // tn),
        in_specs=[pl.BlockSpec((tm, d1), lambda i, j: (i, 0)),
                  pl.BlockSpec((tm, d2), lambda i, j: (i, 0)),
                  pl.BlockSpec((d1, tn), lambda i, j: (0, j)),
                  pl.BlockSpec((d2, tn), lambda i, j: (0, j)),
                  pl.BlockSpec((tm, tn), lambda i, j: (i, b1 + j)),
                  pl.BlockSpec((tm, tn), lambda i, j: (i, b2 + j))],
        out_specs=pl.BlockSpec((tm, tn), lambda i, j: (i, j)),
        out_shape=jax.ShapeDtypeStruct((n, m), BF16),
        compiler_params=_cparams(("parallel", "parallel")),
        name="merge",
    )(y, o, w1, w2, g_src, g_src)


def _mixln_kernel(m_ref, wo_ref, h_ref, g_ref, b_ref, wrh_ref, wrl_ref, br_ref, ht_ref, ids_ref, ew_ref,
                  acc_ref, *, alpha, tn, n_groups, per_group):
    j = pl.program_id(1)
    acc_ref[:, pl.ds(pl.multiple_of(j * tn, tn), tn)] = _dot(m_ref[...], wo_ref[...])

    @pl.when(j == pl.num_programs(1) - 1)
    def _():
        y = _layernorm(alpha * h_ref[...] + acc_ref[...], g_ref[...], b_ref[...])
        tm, d = y.shape
        pitch = ht_ref.shape[0] // tm
        for s in range(pitch):
            chunk = y[:, s * LANES:(s + 1) * LANES] if s < d // LANES else jnp.zeros((tm, LANES), F32)
            ht_ref[pl.ds(s, tm, stride=pitch), :] = chunk
        y_hi, y_lo = _split2(y)
        logits = _dot(y_hi, wrh_ref[...]) + _dot(y_lo, wrh_ref[...]) + _dot(y_hi, wrl_ref[...]) + br_ref[...]
        lane = lax.broadcasted_iota(jnp.int32, logits.shape, 1)
        n_exp = n_groups * per_group
        glog = jnp.where(lane < n_groups, logits, NEG_BIG)
        gmax = jnp.max(glog, axis=-1, keepdims=True)
        gsum = jnp.sum(jnp.exp(glog - gmax), axis=-1, keepdims=True)
        g_val = 1.0 / gsum
        g_idx = jnp.min(jnp.where(glog == gmax, lane, LANES), axis=-1, keepdims=True)
        e_lane = lane - n_groups
        in_grp = (e_lane >= g_idx * per_group) & (e_lane < (g_idx + 1) * per_group) & (e_lane < n_exp)
        sel = jnp.where(in_grp, logits, NEG_BIG)
        m1 = jnp.max(sel, axis=-1, keepdims=True)
        i1 = jnp.min(jnp.where(sel == m1, lane, LANES), axis=-1, keepdims=True)
        sel2 = jnp.where(lane == i1, NEG_BIG, sel)
        m2 = jnp.max(sel2, axis=-1, keepdims=True)
        i2 = jnp.min(jnp.where(sel2 == m2, lane, LANES), axis=-1, keepdims=True)
        e2 = jnp.exp(m2 - m1)
        w1 = g_val / (1.0 + e2)
        w2 = g_val * e2 / (1.0 + e2)
        ids_ref[...] = jnp.where(lane == 0, i1 - n_groups, jnp.where(lane == 1, i2 - n_groups, 0))
        ew_ref[...] = jnp.where(lane == 0, w1, jnp.where(lane == 1, w2, 0.0))


def _slab_pitch(d):
    return d // LANES + SUBLANES


def _mixln(merged, w_o, h, g, b, wr_hi, wr_lo, br, alpha, n_groups, per_group):
    n, d = h.shape
    pitch = _slab_pitch(d)
    tm = _largest_tile(n, 320, 16)
    tn = _largest_tile(d, 512, LANES)
    kern = functools.partial(_mixln_kernel, alpha=alpha, tn=tn, n_groups=n_groups, per_group=per_group)
    row = lambda i, j: (i, 0)
    const = lambda i, j: (0, 0)
    return pl.pallas_call(
        kern,
        grid=(n // tm, d // tn),
        in_specs=[pl.BlockSpec((tm, d), row),
                  pl.BlockSpec((d, tn), lambda i, j: (0, j)),
                  pl.BlockSpec((tm, d), row),
                  pl.BlockSpec((1, d), const), pl.BlockSpec((1, d), const),
                  pl.BlockSpec((d, LANES), const), pl.BlockSpec((d, LANES), const), pl.BlockSpec((1, LANES), const)],
        out_specs=[pl.BlockSpec((tm * pitch, LANES), row),
                   pl.BlockSpec((tm, LANES), row), pl.BlockSpec((tm, LANES), row)],
        out_shape=[jax.ShapeDtypeStruct((n * pitch, LANES), F32),
                   jax.ShapeDtypeStruct((n, LANES), jnp.int32),
                   jax.ShapeDtypeStruct((n, LANES), F32)],
        scratch_shapes=[pltpu.VMEM((tm, d), F32)],
        compiler_params=_cparams(("parallel", "arbitrary")),
        name="mix_ln_router",
    )(merged, w_o, h, g.reshape(1, d), b.reshape(1, d), wr_hi, wr_lo, br)


def _moe_kernel(nused_ref, nvalid_ref, tok_ref, tokn_ref, dst_ref, gw_ref, h_hbm, wg_ref, wu_ref, wd_ref, out_hbm,
                xbuf, a_ref, stage, gsem, ssem, *, tm):
    i = pl.program_id(0)
    j = pl.program_id(1)
    nf = pl.num_programs(1)
    nused = nused_ref[0]
    n_chunks = xbuf.shape[0]
    pitch = stage.shape[0]

    def start_gather(idx_ref):
        def body(r, c):
            pltpu.make_async_copy(h_hbm.at[idx_ref[0, 0, r], pl.ds(0, n_chunks), :], xbuf.at[:, r, :], gsem).start()
            return c
        lax.fori_loop(0, tm, body, 0, unroll=8)

    def wait_gather():
        pltpu.make_async_copy(xbuf, xbuf, gsem).wait()

    def scatter_copy(r):
        dd = dst_ref[0, 0, r]
        return pltpu.make_async_copy(stage.at[:, r, :], out_hbm.at[dd % 2, dd // 2], ssem)

    def for_valid_rows(tile, fn):
        nv = nvalid_ref[tile]

        def eight(g, c):
            for u in range(SUBLANES):
                fn(g * SUBLANES + u)
            return c
        lax.fori_loop(0, nv // SUBLANES, eight, 0)

        def one(r, c):
            fn(r)
            return c
        lax.fori_loop((nv // SUBLANES) * SUBLANES, nv, one, 0)

    def wait_scatter(tile):
        for_valid_rows(tile, lambda r: scatter_copy(0).wait())

    @pl.when((i < nused) & (j == 0))
    def _():
        @pl.when(i == 0)
        def _():
            stage[n_chunks:pitch] = jnp.zeros((pitch - n_chunks, tm, LANES), F32)
            start_gather(tok_ref)

        wait_gather()
        a_ref[...] = jnp.concatenate([xbuf[s] for s in range(n_chunks)], axis=1).astype(BF16)

        @pl.when(i + 1 < nused)
        def _():
            start_gather(tokn_ref)

    @pl.when(i < nused)
    def _():
        a = a_ref[...]
        hg = _dot(a, wg_ref[...])
        hu = _dot(a, wu_ref[...])
        hid = (hg * _sigmoid(hg) * hu * gw_ref[...]).astype(BF16)

        @pl.when((j == 0) & (i > 0))
        def _():
            wait_scatter(i - 1)

        cw = 4 * LANES
        for c0 in range(0, wd_ref.shape[1], cw):
            yc = _dot(hid, wd_ref[:, c0:c0 + cw])

            @pl.when(j == 0)
            def _():
                for s in range(cw // LANES):
                    stage[c0 // LANES + s] = yc[:, s * LANES:(s + 1) * LANES]

            @pl.when(j > 0)
            def _():
                for s in range(cw // LANES):
                    stage[c0 // LANES + s] += yc[:, s * LANES:(s + 1) * LANES]

    @pl.when((i < nused) & (j == nf - 1))
    def _():
        for_valid_rows(i, lambda r: scatter_copy(r).start())

        @pl.when(i == nused - 1)
        def _():
            wait_scatter(i)


def _moe(h_tm, row_tok, row_dst, row_gw, tile_e, nused, tile_nv, wg, wu, wd, tm):
    n_tok, pitch, _ = h_tm.shape
    n_chunks = pitch - SUBLANES
    d = n_chunks * LANES
    n_tiles = row_tok.shape[0] // tm
    f = wg.shape[2]
    tf = _largest_tile(f, 512, LANES)
    kern = functools.partial(_moe_kernel, tm=tm)
    idx3 = row_tok.reshape(n_tiles, 1, tm)
    dst3 = row_dst.reshape(n_tiles, 1, tm)
    smem_blk = lambda im: pl.BlockSpec((1, 1, tm), im, memory_space=pltpu.SMEM)
    grid_spec = pltpu.PrefetchScalarGridSpec(
        num_scalar_prefetch=3,
        grid=(n_tiles, f // tf),
        in_specs=[smem_blk(lambda i, j, te, nu, nv: (i, 0, 0)),
                  smem_blk(lambda i, j, te, nu, nv: (jnp.minimum(i + 1, n_tiles - 1), 0, 0)),
                  smem_blk(lambda i, j, te, nu, nv: (i, 0, 0)),
                  pl.BlockSpec((tm, 1), lambda i, j, te, nu, nv: (i, 0)),
                  pl.BlockSpec(memory_space=pl.ANY),
                  pl.BlockSpec((None, d, tf), lambda i, j, te, nu, nv: (te[i], 0, j)),
                  pl.BlockSpec((None, d, tf), lambda i, j, te, nu, nv: (te[i], 0, j)),
                  pl.BlockSpec((None, tf, d), lambda i, j, te, nu, nv: (te[i], j, 0))],
        out_specs=pl.BlockSpec(memory_space=pl.ANY),
        scratch_shapes=[pltpu.VMEM((n_chunks, tm, LANES), F32), pltpu.VMEM((tm, d), BF16),
                        pltpu.VMEM((pitch, tm, LANES), F32),
                        pltpu.SemaphoreType.DMA(()), pltpu.SemaphoreType.DMA(())],
    )

    def wrapped(te_ref, nu_ref, nv_ref, *rest):
        kern(nu_ref, nv_ref, *rest)

    return pl.pallas_call(
        wrapped,
        grid_spec=grid_spec,
        out_shape=jax.ShapeDtypeStruct((2, n_tok, pitch, LANES), F32),
        compiler_params=pltpu.CompilerParams(dimension_semantics=("arbitrary", "arbitrary"),
                                             vmem_limit_bytes=MOE_VMEM_LIMIT_BYTES),
        name="moe_ffn",
    )(tile_e, nused, tile_nv, idx3, idx3, dst3, row_gw.reshape(-1, 1), h_tm, wg, wu, wd)


def _route_plan(ids, ew, n_experts, tm, n_tiles):
    n = ids.shape[0]
    ea = ids.reshape(-1)
    onehot = (ea[:, None] == jnp.arange(n_experts, dtype=jnp.int32)[None, :]).astype(jnp.int32)
    cs = jnp.cumsum(onehot, axis=0)
    rank = jnp.take_along_axis(cs, ea[:, None], axis=1)[:, 0] - 1
    cnt = cs[-1]
    pcnt = ((cnt + tm - 1) // tm) * tm
    ends = jnp.cumsum(pcnt)
    starts = ends - pcnt
    pos = starts[ea] + rank
    p_rows = n_tiles * tm
    a_idx = jnp.arange(2 * n, dtype=jnp.int32)
    inv = jnp.zeros((p_rows,), jnp.int32).at[pos].set(a_idx + 1)
    row_dst = jnp.maximum(inv - 1, 0)
    row_tok = row_dst // 2
    row_gw = jnp.where(inv > 0, ew.reshape(-1)[row_dst], 0.0)
    nused = (ends[-1] // tm).astype(jnp.int32)
    tile_start = jnp.arange(n_tiles, dtype=jnp.int32) * tm
    tile_e = jnp.sum((tile_start[:, None] >= ends[None, :]).astype(jnp.int32), axis=1)
    tile_e = jnp.minimum(tile_e, n_experts - 1)
    tile_nv = jnp.clip(starts[tile_e] + cnt[tile_e] - tile_start, 0, tm).astype(jnp.int32)
    last_e = tile_e[jnp.maximum(nused - 1, 0)]
    in_use = jnp.arange(n_tiles) < nused
    return (row_tok, row_dst, row_gw, jnp.where(in_use, tile_e, last_e), nused.reshape(1),
            jnp.where(in_use, tile_nv, 0))


def _ln2_kernel(h_ref, m0_ref, m1_ref, g_ref, b_ref, of_ref, ob_ref, *, alpha, pitch):
    tm, d = of_ref.shape
    chunk = lambda ref, s: ref[pl.ds(s, tm, stride=pitch), :]
    x = jnp.concatenate([alpha * chunk(h_ref, s) + (chunk(m0_ref, s) + chunk(m1_ref, s)) for s in range(d // LANES)],
                        axis=1)
    y = _layernorm(x, g_ref[...], b_ref[...])
    of_ref[...] = y
    ob_ref[...] = y.astype(BF16)


def _ln2(h_tm, moe_tm, g, b, alpha):
    d = g.shape[0]
    pitch = _slab_pitch(d)
    n = h_tm.shape[0] // pitch
    tm = _largest_tile(n, 160, 16)
    return pl.pallas_call(
        functools.partial(_ln2_kernel, alpha=alpha, pitch=pitch),
        grid=(n // tm,),
        in_specs=[pl.BlockSpec((tm * pitch, LANES), lambda i: (i, 0)),
                  pl.BlockSpec((None, tm * pitch, LANES), lambda i: (0, i, 0)),
                  pl.BlockSpec((None, tm * pitch, LANES), lambda i: (1, i, 0)),
                  pl.BlockSpec((1, d), lambda i: (0, 0)), pl.BlockSpec((1, d), lambda i: (0, 0))],
        out_specs=[pl.BlockSpec((tm, d), lambda i: (i, 0)), pl.BlockSpec((tm, d), lambda i: (i, 0))],
        out_shape=[jax.ShapeDtypeStruct((n, d), F32), jax.ShapeDtypeStruct((n, d), BF16)],
        compiler_params=_cparams(("parallel",)),
        name="ln2",
    )(h_tm, moe_tm, moe_tm, g.reshape(1, d), b.reshape(1, d))


def kernel(x_prompt, x_sample, cache_k, cache_v, page_table, state_ssm, state_conv, meta_tokens, ln_emb_g, ln_emb_b,
           w_in, conv_w, conv_b, dt_bias, a_log, d_skip, ssd_norm_g, w_ssd_out, w_sb_out, sb_bias, w_o, ln1_g, ln1_b,
           wr_group, br_group, wr_expert, br_expert, w_gate, w_up, w_down, ln2_g, ln2_b):
    n_b, seq, d = x_prompt.shape
    n_db, dseq, _ = x_sample.shape
    depth = w_in.shape[0]
    c_dim = conv_w.shape[-1]
    n_ssd_heads = dt_bias.shape[-1]
    p_dim = d // n_ssd_heads
    nst = (c_dim - d) // (2 * N_SSD_GROUPS)
    sb_heads = sb_bias.shape[-1]
    hd = cache_k.shape[-1]
    sbw = sb_heads * hd
    page = cache_k.shape[2]
    n_exp = w_gate.shape[1]
    per_group = n_exp // N_ROUTE_GROUPS
    alpha = float((2 * depth) ** 0.25)
    n_real = n_b * seq
    n_meta = n_b * N_META
    n_samp = n_db * dseq
    n_tok = n_real + n_meta + n_samp
    assert seq % SSD_CHUNK == 0 and n_ssd_heads <= LANES and N_ROUTE_GROUPS + n_exp <= LANES and d % 512 == 0

    sizes = (d, c_dim, n_ssd_heads, sbw, sbw, sbw, d, d)
    offs = [0]
    for s in sizes:
        offs.append(offs[-1] + s)
    o_z, o_xbc, o_dt, o_q, o_k, o_v, o_gs, o_gb = offs[:8]

    moe_tm = 512 if n_tok >= 4096 else 64
    n_tiles = (2 * n_tok + n_exp * (moe_tm - 1)) // moe_tm + 1

    x_all = jnp.concatenate([x_prompt.reshape(n_real, d),
                             jnp.broadcast_to(meta_tokens[None].astype(x_prompt.dtype), (n_b, N_META, d)).reshape(n_meta, d),
                             x_sample.reshape(n_samp, d)], axis=0)
    h, h_bf = _embed_ln(x_all, ln_emb_g, ln_emb_b)

    lane_pad = lambda v: jnp.pad(v.astype(F32), (0, LANES - v.shape[0])).reshape(1, LANES)
    eexp = (jnp.arange(LANES)[:, None] == (jnp.arange(d)[None, :] // p_dim)).astype(BF16)
    cache_k4 = cache_k.reshape(cache_k.shape[0], depth, page * sb_heads, hd)
    cache_v4 = cache_v.reshape(cache_v.shape[0], depth, page * sb_heads, hd)

    outs = {k: [] for k in ("sp", "cp", "ss", "cs")}
    kv_all = jnp.zeros((depth, n_tok, 2 * sbw), F32)
    for l in range(depth):
        wl = w_in[l]
        seg = lambda c0, width: wl[:, c0:c0 + width].astype(BF16)
        w_dt = jnp.pad(wl[:, o_dt:o_dt + n_ssd_heads], ((0, 0), (0, LANES - n_ssd_heads))).astype(BF16)

        u_z = _matmul(h_bf, seg(o_z, d), BF16, "in_proj_z")
        u_q = _matmul(h_bf, seg(o_q, sbw), BF16, "in_proj_q")
        u_g = _matmul(h_bf, seg(o_gs, 2 * d), BF16, "in_proj_gates")
        u_xbc = _matmul(h_bf, seg(o_xbc, c_dim), F32, "in_proj_xbc")
        u_dt = _matmul(h_bf, w_dt, F32, "in_proj_dt")
        kv_all = _matmul_into_layer(h_bf, seg(o_k, 2 * sbw), kv_all, l, "in_proj_kv")

        consts = (jnp.pad(conv_w[l], ((0, SUBLANES - conv_w.shape[1]), (0, 0))), conv_b[l].reshape(1, c_dim),
                  lane_pad(dt_bias[l]), lane_pad(-jnp.exp(a_log[l].astype(F32))),
                  jnp.repeat(d_skip[l], p_dim).reshape(1, d), ssd_norm_g[l].reshape(1, d), eexp)
        xm = u_xbc[n_real:n_real + n_meta]
        y_meta, st_meta = _ssd_small(xm, jnp.zeros((n_b, SUBLANES, c_dim), F32),
                                     u_z[n_real:n_real + n_meta].astype(F32), u_dt[n_real:n_real + n_meta],
                                     jnp.zeros((n_b, nst, d), F32), consts, n_b, N_META, d, n_ssd_heads, nst, "ssd_meta")
        y_real, st_real = _ssd_main(u_xbc, u_z, u_dt, st_meta, consts, n_b, seq, d, n_ssd_heads, nst)
        xs_ = u_xbc[n_real + n_meta:]
        prev_s = jnp.pad(state_conv[l], ((0, 0), (SUBLANES - state_conv.shape[2], 0), (0, 0)))
        st0_s = jnp.transpose(state_ssm[l].astype(F32), (0, 3, 1, 2)).reshape(n_db, nst, d)
        y_samp, st_samp = _ssd_small(xs_, prev_s, u_z[n_real + n_meta:].astype(F32), u_dt[n_real + n_meta:],
                                     st0_s, consts, n_db, dseq, d, n_ssd_heads, nst, "ssd_sample")
        y_ssd = jnp.concatenate([y_real, y_meta.astype(BF16), y_samp.astype(BF16)], axis=0)

        o_real, o_meta = _sb_prompt(sb_bias[l].astype(F32), u_q, kv_all, l, n_b, seq, sb_heads, hd)
        q_s = u_q[n_real + n_meta:].astype(F32)
        kv_s = kv_all[l, n_real + n_meta:]
        o_samp = _sb_sample(page_table, q_s, kv_s[:, :sbw], kv_s[:, sbw:],
                            jnp.repeat(sb_bias[l].astype(F32), dseq).reshape(1, sb_heads * dseq),
                            cache_k4, cache_v4, l, dseq)
        o_sb = jnp.concatenate([o_real, o_meta, o_samp.astype(BF16)], axis=0)

        merged = _merge(y_ssd, o_sb, w_ssd_out[l].astype(BF16), w_sb_out[l].astype(BF16), u_g, 0, d)
        wr = jnp.concatenate([wr_group[l], jnp.transpose(wr_expert[l], (1, 0, 2)).reshape(d, n_exp)], axis=1)
        wr = jnp.pad(wr.astype(F32), ((0, 0), (0, LANES - wr.shape[1])))
        wr_hi = wr.astype(BF16)
        wr_lo = (wr - wr_hi.astype(F32)).astype(BF16)
        br = lane_pad(jnp.concatenate([br_group[l], br_expert[l].reshape(-1)]))
        h1_tm, ids, ew = _mixln(merged, w_o[l].astype(BF16), h, ln1_g[l], ln1_b[l], wr_hi, wr_lo, br, alpha,
                                N_ROUTE_GROUPS, per_group)
        plan = _route_plan(ids[:, :2], ew[:, :2], n_exp, moe_tm, n_tiles)
        pitch = _slab_pitch(d)
        moe_tm_out = _moe(h1_tm.reshape(n_tok, pitch, LANES), *plan, w_gate[l].astype(BF16), w_up[l].astype(BF16),
                          w_down[l].astype(BF16), moe_tm)
        h, h_bf = _ln2(h1_tm, moe_tm_out.reshape(2, n_tok * pitch, LANES), ln2_g[l], ln2_b[l], alpha)

        to_state = lambda st, nb: jnp.transpose(st.reshape(nb, nst, n_ssd_heads, p_dim), (0, 2, 3, 1))
        outs["sp"].append(to_state(st_real, n_b)); outs["ss"].append(to_state(st_samp, n_db))
        outs["cp"].append(jnp.stack([u_xbc[(b + 1) * seq - 3:(b + 1) * seq] for b in range(n_b)]))
        outs["cs"].append(xs_.reshape(n_db, dseq, c_dim)[:, dseq - 3:, :])

    def kv_out(col0):
        real = kv_all[:, :n_real, col0:col0 + sbw].reshape(depth, n_b, seq, sb_heads, hd)
        meta = kv_all[:, n_real:n_real + n_meta, col0:col0 + sbw].reshape(depth, n_b, N_META, sb_heads, hd)
        samp = kv_all[:, n_real + n_meta:, col0:col0 + sbw].reshape(depth, n_db, dseq, sb_heads, hd)
        return jnp.concatenate([meta, real], axis=2), samp
    kp, ks = kv_out(0)
    vp, vs = kv_out(sbw)
    st = lambda k: jnp.stack(outs[k])
    return (h[:n_real].reshape(n_b, seq, d), h[n_real + n_meta:].reshape(n_db, dseq, d),
            kp, vp, st("sp"), st("cp"), ks, vs, st("ss"), st("cs"))
```

```python
import functools
import math

import jax
import jax.numpy as jnp
from jax import lax
from jax.experimental import pallas as pl
from jax.experimental.pallas import tpu as pltpu

F32 = jnp.float32
BF16 = jnp.bfloat16

N_META = 16
SSD_CHUNK = 128
N_SSD_GROUPS = 8
N_ROUTE_GROUPS = 4
LN_EPS = 1e-5
RMS_EPS = 1e-5
NEG_BIG = -1e30

LANES = 128
SUBLANES = 8
VMEM_LIMIT_BYTES = 56 * 1024 * 1024
MOE_VMEM_LIMIT_BYTES = 60 * 1024 * 1024


def _cparams(sem):
    return pltpu.CompilerParams(dimension_semantics=sem, vmem_limit_bytes=VMEM_LIMIT_BYTES)


def _largest_tile(n, cap, mult):
    best = None
    for t in range(mult, min(n, cap) + 1, mult):
        if n % t == 0:
            best = t
    assert best is not None, (n, cap, mult)
    return best


def _dot(a, b):
    return jnp.dot(a, b, preferred_element_type=F32)


def _dot_nt(a, b):
    return lax.dot_general(a, b, (((1,), (1,)), ((), ())), preferred_element_type=F32)


def _dot_tn(a, b):
    return lax.dot_general(a, b, (((0,), (0,)), ((), ())), preferred_element_type=F32)


def _split3(x):
    hi = x.astype(BF16)
    r = x - hi.astype(F32)
    mid = r.astype(BF16)
    lo = (r - mid.astype(F32)).astype(BF16)
    return hi, mid, lo


def _split2(x):
    hi = x.astype(BF16)
    lo = (x - hi.astype(F32)).astype(BF16)
    return hi, lo


def _softplus(x):
    return jnp.maximum(x, 0.0) + jnp.log(1.0 + jnp.exp(-jnp.abs(x)))


def _sigmoid(x):
    return 1.0 / (1.0 + jnp.exp(-x))


def _layernorm(x, g, b):
    mu = jnp.mean(x, axis=-1, keepdims=True)
    xc = x - mu
    var = jnp.mean(xc * xc, axis=-1, keepdims=True)
    return xc * lax.rsqrt(var + LN_EPS) * g + b


def _ln_kernel(x_ref, g_ref, b_ref, of_ref, ob_ref):
    y = _layernorm(x_ref[...], g_ref[...], b_ref[...])
    of_ref[...] = y
    ob_ref[...] = y.astype(BF16)


def _embed_ln(x, g, b):
    n, d = x.shape
    tm = _largest_tile(n, 320, 16)
    return pl.pallas_call(
        _ln_kernel,
        grid=(n // tm,),
        in_specs=[pl.BlockSpec((tm, d), lambda i: (i, 0)),
                  pl.BlockSpec((1, d), lambda i: (0, 0)),
                  pl.BlockSpec((1, d), lambda i: (0, 0))],
        out_specs=[pl.BlockSpec((tm, d), lambda i: (i, 0)),
                   pl.BlockSpec((tm, d), lambda i: (i, 0))],
        out_shape=[jax.ShapeDtypeStruct((n, d), F32), jax.ShapeDtypeStruct((n, d), BF16)],
        compiler_params=_cparams(("parallel",)),
        name="embed_ln",
    )(x, g.reshape(1, d), b.reshape(1, d))


def _mm_kernel(x_ref, w_ref, o_ref):
    o_ref[...] = _dot(x_ref[...], w_ref[...]).astype(o_ref.dtype)


def _matmul(x, w, out_dtype, name):
    n, k = x.shape
    m = w.shape[1]
    tm = _largest_tile(n, 640, 16)
    tn = _largest_tile(m, 1024, LANES)
    return pl.pallas_call(
        _mm_kernel,
        grid=(n // tm, m // tn),
        in_specs=[pl.BlockSpec((tm, k), lambda i, j: (i, 0)),
                  pl.BlockSpec((k, tn), lambda i, j: (0, j))],
        out_specs=pl.BlockSpec((tm, tn), lambda i, j: (i, j)),
        out_shape=jax.ShapeDtypeStruct((n, m), out_dtype),
        compiler_params=_cparams(("parallel", "parallel")),
        name=name,
    )(x, w)


def _mm_into_kernel(x_ref, w_ref, buf_ref, o_ref):
    del buf_ref
    o_ref[...] = _dot(x_ref[...], w_ref[...]).astype(o_ref.dtype)


def _matmul_into_layer(x, w, buf, layer, name):
    n, k = x.shape
    m = w.shape[1]
    tm = _largest_tile(n, 640, 16)
    tn = _largest_tile(m, 1024, LANES)
    return pl.pallas_call(
        _mm_into_kernel,
        grid=(n // tm, m // tn),
        in_specs=[pl.BlockSpec((tm, k), lambda i, j: (i, 0)), pl.BlockSpec((k, tn), lambda i, j: (0, j)),
                  pl.BlockSpec(memory_space=pl.ANY)],
        out_specs=pl.BlockSpec((None, tm, tn), lambda i, j: (layer, i, j)),
        out_shape=jax.ShapeDtypeStruct(buf.shape, buf.dtype),
        input_output_aliases={2: 0},
        compiler_params=_cparams(("parallel", "parallel")),
        name=name,
    )(x, w, buf)


def _ssd_chunk(xpad_ref, act_ref, st_ref, z, dtraw, n_valid, cw_ref, cb_ref, dtb_ref, aneg_ref, dskip_ref,
               ng_ref, eexp_ref, y_write, d_model, n_heads):
    q = SSD_CHUNK
    c_dim = xpad_ref.shape[1]
    nst = st_ref.shape[0]
    g_cnt = N_SSD_GROUPS
    gw = d_model // g_cnt
    heads_per_group = n_heads // g_cnt
    b_off = d_model
    c_off = d_model + g_cnt * nst

    ct = 512
    for c0 in range(0, c_dim, ct):
        acc = cb_ref[:, c0:c0 + ct]
        for i in range(4):
            acc = acc + xpad_ref[pl.ds(SUBLANES - 3 + i, q), c0:c0 + ct] * cw_ref[i:i + 1, c0:c0 + ct]
        act_ref[:, c0:c0 + ct] = acc * _sigmoid(acc)

    rows = lax.broadcasted_iota(jnp.int32, (q, q), 0)
    cols = lax.broadcasted_iota(jnp.int32, (q, q), 1)
    tri = cols <= rows
    tril_ones = jnp.where(tri, 1.0, 0.0).astype(BF16)
    low_half = cols < (LANES // 2)

    dt = _softplus(dtraw + dtb_ref[...])
    if n_valid < q:
        dt = jnp.where(rows < n_valid, dt, 0.0)
    da = dt * aneg_ref[...]
    d_hi, d_mid, d_lo = _split3(da)
    acum = _dot(tril_ones, d_hi) + _dot(tril_ones, d_mid) + _dot(tril_ones, d_lo)
    last = acum[q - 1:q, :]
    w_end = jnp.exp(last - acum) * dt
    acum_t = acum.T
    dt_t = dt.T
    eexp = eexp_ref[...]
    w_exp = _dot(w_end.astype(BF16), eexp)
    cd = jnp.broadcast_to(jnp.exp(last), (SUBLANES, LANES))
    cd_hi, cd_lo = _split2(cd)
    cd_exp = (_dot(cd_hi, eexp) + _dot(cd_lo, eexp))[0:1, :]

    for g in range(g_cnt):
        bm = act_ref[:, b_off + g * nst:b_off + (g + 1) * nst]
        cm = act_ref[:, c_off + g * nst:c_off + (g + 1) * nst]
        bm_b = bm.astype(BF16)
        cb_mat = _dot_nt(cm.astype(BF16), bm_b)
        ys = []
        for pr in range(heads_per_group // 2):
            col0 = g * gw + pr * LANES
            lhs = []
            for hh in range(2):
                h = g * heads_per_group + 2 * pr + hh
                colb = jnp.broadcast_to(acum[:, h:h + 1], (q, q))
                seg = colb - acum_t[h:h + 1, :]
                dec = jnp.exp(jnp.where(tri, seg, NEG_BIG))
                lhs.append((cb_mat * dec * dt_t[h:h + 1, :]).astype(BF16))
                lhs.append((cm * jnp.exp(colb)).astype(BF16))
            xp = act_ref[:, col0:col0 + LANES]
            sp = st_ref[:, col0:col0 + LANES]
            rhs = jnp.concatenate([jnp.where(low_half, xp, 0.0), jnp.where(low_half, sp, 0.0),
                                   jnp.where(low_half, 0.0, xp), jnp.where(low_half, 0.0, sp)], axis=0).astype(BF16)
            yp = _dot(jnp.concatenate(lhs, axis=1), rhs)
            yp = yp + dskip_ref[:, col0:col0 + LANES] * xp
            zp = z[:, col0:col0 + LANES]
            ys.append(yp * (zp * _sigmoid(zp)))
        ss = ys[0] * ys[0]
        for yp in ys[1:]:
            ss = ss + yp * yp
        inv = lax.rsqrt(jnp.sum(ss, axis=-1, keepdims=True) / gw + RMS_EPS)
        for pr, yp in enumerate(ys):
            col0 = g * gw + pr * LANES
            y_write(col0, yp * inv * ng_ref[:, col0:col0 + LANES])
        xg = act_ref[:, g * gw:(g + 1) * gw]
        xw = (xg * w_exp[:, g * gw:(g + 1) * gw]).astype(BF16)
        dst = _dot_tn(bm_b, xw)
        st_ref[:, g * gw:(g + 1) * gw] = st_ref[:, g * gw:(g + 1) * gw] * cd_exp[:, g * gw:(g + 1) * gw] + dst


def _ssd_main_kernel(xbc_ref, xmeta_ref, z_ref, dt_ref, st0_ref, cw_ref, cb_ref, dtb_ref, aneg_ref, dskip_ref,
                     ng_ref, eexp_ref, y_ref, stout_ref, xpad_ref, act_ref, st_ref, *, d_model, n_heads):
    c = pl.program_id(1)

    @pl.when(c == 0)
    def _():
        st_ref[...] = st0_ref[0]
        xpad_ref[0:SUBLANES, :] = xmeta_ref[N_META - SUBLANES:N_META, :]

    xpad_ref[SUBLANES:SUBLANES + SSD_CHUNK, :] = xbc_ref[...]

    def y_write(col0, val):
        y_ref[:, col0:col0 + val.shape[1]] = val.astype(y_ref.dtype)

    _ssd_chunk(xpad_ref, act_ref, st_ref, z_ref[...].astype(F32), dt_ref[...], SSD_CHUNK, cw_ref, cb_ref, dtb_ref,
               aneg_ref, dskip_ref, ng_ref, eexp_ref, y_write, d_model, n_heads)
    xpad_ref[0:SUBLANES, :] = xbc_ref[SSD_CHUNK - SUBLANES:SSD_CHUNK, :]

    @pl.when(c == pl.num_programs(1) - 1)
    def _():
        stout_ref[0] = st_ref[...]


def _ssd_small_kernel(x_ref, prev_ref, z_ref, dt_ref, st0_ref, cw_ref, cb_ref, dtb_ref, aneg_ref, dskip_ref,
                      ng_ref, eexp_ref, y_ref, stout_ref, xpad_ref, act_ref, st_ref, *, d_model, n_heads, seq):
    st_ref[...] = st0_ref[0]
    xpad_ref[0:SUBLANES, :] = prev_ref[0]
    xpad_ref[SUBLANES:SUBLANES + seq, :] = x_ref[...]
    xpad_ref[SUBLANES + seq:, :] = jnp.zeros((SSD_CHUNK - seq, xpad_ref.shape[1]), F32)
    pad = jnp.zeros((SSD_CHUNK - seq, d_model), F32)
    z = jnp.concatenate([z_ref[...], pad], axis=0)
    dtr = jnp.concatenate([dt_ref[...], jnp.zeros((SSD_CHUNK - seq, LANES), F32)], axis=0)

    def y_write(col0, val):
        y_ref[:, col0:col0 + val.shape[1]] = val[0:seq, :]

    _ssd_chunk(xpad_ref, act_ref, st_ref, z, dtr, seq, cw_ref, cb_ref, dtb_ref, aneg_ref, dskip_ref, ng_ref,
               eexp_ref, y_write, d_model, n_heads)
    stout_ref[0] = st_ref[...]


def _ssd_consts_specs(c_dim, d_model, idx):
    return [pl.BlockSpec((SUBLANES, c_dim), idx), pl.BlockSpec((1, c_dim), idx), pl.BlockSpec((1, LANES), idx),
            pl.BlockSpec((1, LANES), idx), pl.BlockSpec((1, d_model), idx), pl.BlockSpec((1, d_model), idx),
            pl.BlockSpec((LANES, d_model), idx)]


def _ssd_main(xbc, z_src, dt, st0, consts, n_batch, seq, d_model, n_heads, nst):
    c_dim = xbc.shape[1]
    nc = seq // SSD_CHUNK
    meta_blk0 = n_batch * seq // N_META
    kern = functools.partial(_ssd_main_kernel, d_model=d_model, n_heads=n_heads)
    return pl.pallas_call(
        kern,
        grid=(n_batch, nc),
        in_specs=[pl.BlockSpec((SSD_CHUNK, c_dim), lambda b, c: (b * nc + c, 0)),
                  pl.BlockSpec((N_META, c_dim), lambda b, c: (meta_blk0 + b, 0)),
                  pl.BlockSpec((SSD_CHUNK, d_model), lambda b, c: (b * nc + c, 0)),
                  pl.BlockSpec((SSD_CHUNK, LANES), lambda b, c: (b * nc + c, 0)),
                  pl.BlockSpec((1, nst, d_model), lambda b, c: (b, 0, 0))]
        + _ssd_consts_specs(c_dim, d_model, lambda b, c: (0, 0)),
        out_specs=[pl.BlockSpec((SSD_CHUNK, d_model), lambda b, c: (b * nc + c, 0)),
                   pl.BlockSpec((1, nst, d_model), lambda b, c: (b, 0, 0))],
        out_shape=[jax.ShapeDtypeStruct((n_batch * seq, d_model), BF16),
                   jax.ShapeDtypeStruct((n_batch, nst, d_model), F32)],
        scratch_shapes=[pltpu.VMEM((SUBLANES + SSD_CHUNK, c_dim), F32), pltpu.VMEM((SSD_CHUNK, c_dim), F32),
                        pltpu.VMEM((nst, d_model), F32)],
        compiler_params=_cparams(("parallel", "arbitrary")),
        name="ssd_main",
    )(xbc, xbc, z_src, dt, st0, *consts)


def _ssd_small(x, prev, z, dt, st0, consts, n_seq, seq, d_model, n_heads, nst, name):
    c_dim = x.shape[1]
    kern = functools.partial(_ssd_small_kernel, d_model=d_model, n_heads=n_heads, seq=seq)
    return pl.pallas_call(
        kern,
        grid=(n_seq,),
        in_specs=[pl.BlockSpec((seq, c_dim), lambda i: (i, 0)),
                  pl.BlockSpec((1, SUBLANES, c_dim), lambda i: (i, 0, 0)),
                  pl.BlockSpec((seq, d_model), lambda i: (i, 0)),
                  pl.BlockSpec((seq, LANES), lambda i: (i, 0)),
                  pl.BlockSpec((1, nst, d_model), lambda i: (i, 0, 0))]
        + _ssd_consts_specs(c_dim, d_model, lambda i: (0, 0)),
        out_specs=[pl.BlockSpec((seq, d_model), lambda i: (i, 0)),
                   pl.BlockSpec((1, nst, d_model), lambda i: (i, 0, 0))],
        out_shape=[jax.ShapeDtypeStruct((n_seq * seq, d_model), F32),
                   jax.ShapeDtypeStruct((n_seq, nst, d_model), F32)],
        scratch_shapes=[pltpu.VMEM((SUBLANES + SSD_CHUNK, c_dim), F32), pltpu.VMEM((SSD_CHUNK, c_dim), F32),
                        pltpu.VMEM((nst, d_model), F32)],
        compiler_params=_cparams(("parallel",)),
        name=name,
    )(x, prev, z, dt, st0, *consts)


SB_CLAMP = 80.0
SB_PAGES_PER_STEP = 4
SB_KEY_TILE = 256
SB_QUERY_TILE = 512


def _suffix_scan_matrix(nk):
    r = lax.broadcasted_iota(jnp.int32, (nk + SUBLANES, nk), 0)
    c = lax.broadcasted_iota(jnp.int32, (nk + SUBLANES, nk), 1)
    return jnp.where((c > r) | (r >= nk), 1.0, 0.0).astype(BF16)


def _sb_tile(s_t, bias, scale, visible, run, scan_mat):
    return _sb_weights(_sb_scan(_sb_logits(s_t, bias, scale, visible), scan_mat), run)


def _sb_logits(s_t, bias, scale, visible):
    z = s_t * scale + bias
    t = jnp.minimum(z, SB_CLAMP)
    lg = jnp.log(1.0 + jnp.exp(t))
    lneg = (t - z) - lg
    vis = None
    if visible is not None:
        vis = visible(lax.broadcasted_iota(jnp.int32, s_t.shape, 0), lax.broadcasted_iota(jnp.int32, s_t.shape, 1))
        lneg = jnp.where(vis, lneg, 0.0)
    return (t - lg, vis) + _split2(lneg)


def _sb_scan(logits, scan_mat):
    logsig, vis, l_hi, l_lo = logits
    return logsig, vis, _dot(scan_mat, l_hi) + _dot(scan_mat, l_lo)


def _sb_weights(scanned, run):
    logsig, vis, sc = scanned
    nk = logsig.shape[0]
    w = jnp.exp(logsig + (sc[0:nk, :] + run))
    if vis is not None:
        w = jnp.where(vis, w, 0.0)
    return w, run + sc[nk:nk + 1, :]


def _sb_prompt_kernel(bias_ref, q_ref, qm_ref, k_ref, km_ref, v_ref, vm_ref, o_ref, om_ref, kb_ref, vt_ref,
                      *, seq, tq, tk, scale):
    h = pl.program_id(1)
    bias = bias_ref[h]
    n_t = seq // tq
    n_diag = tq // tk
    hd = k_ref.shape[1]
    for t in range(seq // tk):
        kb_ref[t * tk:(t + 1) * tk, :] = k_ref[t * tk:(t + 1) * tk, :].astype(BF16)
        vt_ref[:, t * tk:(t + 1) * tk] = v_ref[t * tk:(t + 1) * tk, :].T.astype(BF16)
    zpad = jnp.zeros((LANES - N_META, hd), F32)
    km = jnp.concatenate([km_ref[...], zpad], axis=0).astype(BF16)
    vm_t = jnp.concatenate([vm_ref[...], zpad], axis=0).T.astype(BF16)
    scan_t = _suffix_scan_matrix(tk)
    scan_m = _suffix_scan_matrix(LANES)

    def diag_vis(off):
        return lambda krow, qcol: krow + off < qcol

    def meta_vis(krow, qcol):
        return krow < N_META

    def q_tile(i, carry):
        q0 = pl.multiple_of(i * tq, tq)
        qt = q_ref[pl.ds(q0, tq), :]

        def k_tiles(k0s, visibles, c2):
            o_acc, run2 = c2
            logits = [_sb_logits(_dot_nt(kb_ref[pl.ds(k0, tk), :], qt), bias, scale, vis)
                      for k0, vis in zip(k0s, visibles)]
            scanned = [_sb_scan(lg, scan_t) for lg in logits]
            for k0, sc in zip(k0s, scanned):
                w2, run2 = _sb_weights(sc, run2)
                o_acc = o_acc + _dot(vt_ref[:, pl.ds(k0, tk)], w2.astype(BF16))
            return o_acc, run2

        sc_m = _sb_scan(_sb_logits(_dot_nt(km, qt), bias, scale, meta_vis), scan_m)
        offs = [u * tk for u in reversed(range(n_diag))]
        o_t, run = k_tiles([pl.multiple_of(q0 + off, tk) for off in offs], [diag_vis(off) for off in offs],
                           (jnp.zeros((hd, tq), F32), jnp.zeros((1, tq), F32)))
        o_t, run = lax.fori_loop(
            0, i,
            lambda pp, c2: k_tiles([pl.multiple_of((i - 1 - pp) * tq + u * tk, tk) for u in reversed(range(n_diag))],
                                   [None] * n_diag, c2),
            (o_t, run))
        w3, run = _sb_weights(sc_m, run)
        o_t = o_t + _dot(vm_t, w3.astype(BF16))
        o_ref[pl.ds(q0, tq), :] = o_t.T.astype(o_ref.dtype)
        return carry

    lax.fori_loop(0, n_t, q_tile, 0)

    qm = jnp.concatenate([qm_ref[...], jnp.zeros((LANES - N_META, hd), qm_ref.dtype)], axis=0)
    s_m = _dot_nt(km, qm)
    w_m, _ = _sb_tile(s_m, bias, scale, diag_vis(0), jnp.zeros((1, LANES), F32), scan_m)
    o_m = _dot(vm_t, w_m.astype(BF16)).T
    om_ref[...] = o_m[0:N_META, :].astype(om_ref.dtype)


def _sb_prompt(sb_bias, q_src, kv_all, layer, n_batch, seq, n_heads, hd):
    tk = _largest_tile(seq, SB_KEY_TILE, LANES)
    tq = _largest_tile(seq, SB_QUERY_TILE, tk)
    mb0 = n_batch * seq // N_META
    kern = functools.partial(_sb_prompt_kernel, seq=seq, tq=tq, tk=tk, scale=hd ** -0.5)
    return pl.pallas_call(
        kern,
        grid=(n_batch, n_heads),
        in_specs=[pl.BlockSpec(memory_space=pltpu.SMEM),
                  pl.BlockSpec((seq, hd), lambda b, h: (b, h)),
                  pl.BlockSpec((N_META, hd), lambda b, h: (mb0 + b, h)),
                  pl.BlockSpec((None, seq, hd), lambda b, h: (layer, b, h)),
                  pl.BlockSpec((None, N_META, hd), lambda b, h: (layer, mb0 + b, h)),
                  pl.BlockSpec((None, seq, hd), lambda b, h: (layer, b, n_heads + h)),
                  pl.BlockSpec((None, N_META, hd), lambda b, h: (layer, mb0 + b, n_heads + h))],
        out_specs=[pl.BlockSpec((seq, hd), lambda b, h: (b, h)),
                   pl.BlockSpec((N_META, hd), lambda b, h: (b, h))],
        out_shape=[jax.ShapeDtypeStruct((n_batch * seq, n_heads * hd), BF16),
                   jax.ShapeDtypeStruct((n_batch * N_META, n_heads * hd), BF16)],
        scratch_shapes=[pltpu.VMEM((seq, hd), BF16), pltpu.VMEM((hd, seq), BF16)],
        compiler_params=_cparams(("parallel", "parallel")),
        name="sb_prompt",
    )(sb_bias, q_src, q_src, kv_all, kv_all, kv_all, kv_all)


def _sb_sample_kernel(pt_ref, q_ref, kn_ref, vn_ref, bias_ref, *refs, n_heads, hd, dseq, page, scale):
    cache_refs, (o_ref, qbd_ref, acc_ref, run_ref) = refs[:-4], refs[-4:]
    p = pl.program_id(1)
    width = n_heads * hd
    nq = n_heads * dseq
    bias = bias_ref[...]
    scan_p = _suffix_scan_matrix(page)

    def new_vis(krow, lane):
        return krow < lane % dseq

    def attend(k_blk, v_blk, visible, run):
        s_t = _dot_nt(k_blk, qbd_ref[...])
        w, run = _sb_tile(s_t, bias, scale, visible, run, scan_p)
        return _dot(w.T.astype(BF16), v_blk), run

    @pl.when(p == 0)
    def _():
        q_t = jnp.concatenate([q_ref[...]] * n_heads, axis=0)
        r = lax.broadcasted_iota(jnp.int32, (nq, width), 0) // dseq
        c = lax.broadcasted_iota(jnp.int32, (nq, width), 1) // hd
        qbd_ref[...] = jnp.where(r == c, q_t, 0.0).astype(BF16)
        zpad = jnp.zeros((page - dseq, width), F32)
        k_new = jnp.concatenate([kn_ref[...], zpad], axis=0).astype(BF16)
        v_new = jnp.concatenate([vn_ref[...], zpad], axis=0).astype(BF16)
        acc_ref[...], run_ref[...] = attend(k_new, v_new, new_vis, jnp.zeros(run_ref.shape, F32))

    def page_rows(ref):
        return jnp.concatenate([ref[pl.ds(hh, page, stride=n_heads), :] for hh in range(n_heads)],
                               axis=1).astype(BF16)

    n_pg = len(cache_refs) // 2
    qbd = qbd_ref[...]
    logits = [_sb_logits(_dot_nt(page_rows(kc_ref), qbd), bias, scale, None) for kc_ref in cache_refs[:n_pg]]
    scanned = [_sb_scan(lg, scan_p) for lg in logits]
    run = run_ref[...]
    contrib = None
    for sc, vc_ref in zip(scanned, cache_refs[n_pg:]):
        w, run = _sb_weights(sc, run)
        c = _dot(w.T.astype(BF16), page_rows(vc_ref))
        contrib = c if contrib is None else contrib + c
    acc_ref[...] += contrib
    run_ref[...] = run

    @pl.when(p == pl.num_programs(1) - 1)
    def _():
        o_ref[...] = jnp.concatenate(
            [acc_ref[hh * dseq:(hh + 1) * dseq, hh * hd:(hh + 1) * hd] for hh in range(n_heads)], axis=1)


def _sb_sample(page_table, q, k_new, v_new, bias_lane, cache_k, cache_v, layer, dseq):
    n_db, n_pages = page_table.shape
    hd = cache_k.shape[-1]
    width = q.shape[1]
    n_heads = width // hd
    page = cache_k.shape[2] // n_heads
    nq = n_heads * dseq
    assert nq <= LANES and dseq == SUBLANES, (n_heads, dseq)
    kern = functools.partial(_sb_sample_kernel, n_heads=n_heads, hd=hd, dseq=dseq, page=page, scale=hd ** -0.5)
    pps = _largest_tile(n_pages, SB_PAGES_PER_STEP, 1)

    def cache_spec(k):
        return pl.BlockSpec((None, None, page * n_heads, hd),
                            lambda b, p, pt: (pt[b, n_pages - 1 - (p * pps + k)], layer, 0, 0))

    grid_spec = pltpu.PrefetchScalarGridSpec(
        num_scalar_prefetch=1,
        grid=(n_db, n_pages // pps),
        in_specs=[pl.BlockSpec((dseq, width), lambda b, p, pt: (b, 0)),
                  pl.BlockSpec((dseq, width), lambda b, p, pt: (b, 0)),
                  pl.BlockSpec((dseq, width), lambda b, p, pt: (b, 0)),
                  pl.BlockSpec((1, nq), lambda b, p, pt: (0, 0))]
        + [cache_spec(k) for k in range(pps)] * 2,
        out_specs=pl.BlockSpec((dseq, width), lambda b, p, pt: (b, 0)),
        scratch_shapes=[pltpu.VMEM((nq, width), BF16), pltpu.VMEM((nq, width), F32), pltpu.VMEM((1, nq), F32)],
    )
    return pl.pallas_call(
        kern,
        grid_spec=grid_spec,
        out_shape=jax.ShapeDtypeStruct((n_db * dseq, width), F32),
        compiler_params=_cparams(("parallel", "arbitrary")),
        name="sb_sample",
    )(page_table, q, k_new, v_new, bias_lane, *([cache_k] * pps + [cache_v] * pps))


def _merge_kernel(y_ref, o_ref, w1_ref, w2_ref, g1_ref, g2_ref, out_ref):
    a = _dot(y_ref[...], w1_ref[...])
    b = _dot(o_ref[...], w2_ref[...])
    out_ref[...] = (_sigmoid(g1_ref[...].astype(F32)) * a + _sigmoid(g2_ref[...].astype(F32)) * b).astype(BF16)


def _merge(y, o, w1, w2, g_src, g1_col0, g2_col0):
    n, d1 = y.shape
    d2 = o.shape[1]
    m = w1.shape[1]
    tm = _largest_tile(n, 640, 16)
    tn = _largest_tile(math.gcd(math.gcd(m, g1_col0), g2_col0), 512, LANES)
    b1, b2 = g1_col0 // tn, g2_col0 // tn
    return pl.pallas_call(
        _merge_kernel,
        grid=(n // tm, m // tn),
        in_specs=[pl.BlockSpec((tm, d1), lambda i, j: (i, 0)),
                  pl.BlockSpec((tm, d2), lambda i, j: (i, 0)),
                  pl.BlockSpec((d1, tn), lambda i, j: (0, j)),
                  pl.BlockSpec((d2, tn), lambda i, j: (0, j)),
                  pl.BlockSpec((tm, tn), lambda i, j: (i, b1 + j)),
                  pl.BlockSpec((tm, tn), lambda i, j: (i, b2 + j))],
        out_specs=pl.BlockSpec((tm, tn), lambda i, j: (i, j)),
        out_shape=jax.ShapeDtypeStruct((n, m), BF16),
        compiler_params=_cparams(("parallel", "parallel")),
        name="merge",
    )(y, o, w1, w2, g_src, g_src)


def _mixln_kernel(m_ref, wo_ref, h_ref, g_ref, b_ref, wrh_ref, wrl_ref, br_ref, ht_ref, ids_ref, ew_ref,
                  acc_ref, *, alpha, tn, n_groups, per_group):
    j = pl.program_id(1)
    acc_ref[:, pl.ds(pl.multiple_of(j * tn, tn), tn)] = _dot(m_ref[...], wo_ref[...])

    @pl.when(j == pl.num_programs(1) - 1)
    def _():
        y = _layernorm(alpha * h_ref[...] + acc_ref[...], g_ref[...], b_ref[...])
        tm, d = y.shape
        pitch = ht_ref.shape[0] // tm
        for s in range(pitch):
            chunk = y[:, s * LANES:(s + 1) * LANES] if s < d // LANES else jnp.zeros((tm, LANES), F32)
            ht_ref[pl.ds(s, tm, stride=pitch), :] = chunk
        y_hi, y_lo = _split2(y)
        logits = _dot(y_hi, wrh_ref[...]) + _dot(y_lo, wrh_ref[...]) + _dot(y_hi, wrl_ref[...]) + br_ref[...]
        lane = lax.broadcasted_iota(jnp.int32, logits.shape, 1)
        n_exp = n_groups * per_group
        glog = jnp.where(lane < n_groups, logits, NEG_BIG)
        gmax = jnp.max(glog, axis=-1, keepdims=True)
        gsum = jnp.sum(jnp.exp(glog - gmax), axis=-1, keepdims=True)
        g_val = 1.0 / gsum
        g_idx = jnp.min(jnp.where(glog == gmax, lane, LANES), axis=-1, keepdims=True)
        e_lane = lane - n_groups
        in_grp = (e_lane >= g_idx * per_group) & (e_lane < (g_idx + 1) * per_group) & (e_lane < n_exp)
        sel = jnp.where(in_grp, logits, NEG_BIG)
        m1 = jnp.max(sel, axis=-1, keepdims=True)
        i1 = jnp.min(jnp.where(sel == m1, lane, LANES), axis=-1, keepdims=True)
        sel2 = jnp.where(lane == i1, NEG_BIG, sel)
        m2 = jnp.max(sel2, axis=-1, keepdims=True)
        i2 = jnp.min(jnp.where(sel2 == m2, lane, LANES), axis=-1, keepdims=True)
        e2 = jnp.exp(m2 - m1)
        w1 = g_val / (1.0 + e2)
        w2 = g_val * e2 / (1.0 + e2)
        ids_ref[...] = jnp.where(lane == 0, i1 - n_groups, jnp.where(lane == 1, i2 - n_groups, 0))
        ew_ref[...] = jnp.where(lane == 0, w1, jnp.where(lane == 1, w2, 0.0))


def _slab_pitch(d):
    return d // LANES + SUBLANES


def _mixln(merged, w_o, h, g, b, wr_hi, wr_lo, br, alpha, n_groups, per_group):
    n, d = h.shape
    pitch = _slab_pitch(d)
    tm = _largest_tile(n, 320, 16)
    tn = _largest_tile(d, 512, LANES)
    kern = functools.partial(_mixln_kernel, alpha=alpha, tn=tn, n_groups=n_groups, per_group=per_group)
    row = lambda i, j: (i, 0)
    const = lambda i, j: (0, 0)
    return pl.pallas_call(
        kern,
        grid=(n // tm, d // tn),
        in_specs=[pl.BlockSpec((tm, d), row),
                  pl.BlockSpec((d, tn), lambda i, j: (0, j)),
                  pl.BlockSpec((tm, d), row),
                  pl.BlockSpec((1, d), const), pl.BlockSpec((1, d), const),
                  pl.BlockSpec((d, LANES), const), pl.BlockSpec((d, LANES), const), pl.BlockSpec((1, LANES), const)],
        out_specs=[pl.BlockSpec((tm * pitch, LANES), row),
                   pl.BlockSpec((tm, LANES), row), pl.BlockSpec((tm, LANES), row)],
        out_shape=[jax.ShapeDtypeStruct((n * pitch, LANES), F32),
                   jax.ShapeDtypeStruct((n, LANES), jnp.int32),
                   jax.ShapeDtypeStruct((n, LANES), F32)],
        scratch_shapes=[pltpu.VMEM((tm, d), F32)],
        compiler_params=_cparams(("parallel", "arbitrary")),
        name="mix_ln_router",
    )(merged, w_o, h, g.reshape(1, d), b.reshape(1, d), wr_hi, wr_lo, br)


def _moe_kernel(nused_ref, nvalid_ref, tok_ref, tokn_ref, dst_ref, gw_ref, h_hbm, wg_ref, wu_ref, wd_ref, out_hbm,
                xbuf, a_ref, stage, gsem, ssem, *, tm):
    i = pl.program_id(0)
    j = pl.program_id(1)
    nf = pl.num_programs(1)
    nused = nused_ref[0]
    n_chunks = xbuf.shape[0]
    pitch = stage.shape[0]

    def start_gather(idx_ref):
        def eight(g, c):
            for u in range(SUBLANES):
                r = g * SUBLANES + u
                pltpu.make_async_copy(h_hbm.at[idx_ref[0, 0, r], pl.ds(0, n_chunks), :], xbuf.at[:, r, :],
                                      gsem).start(priority=u % 2)
            return c
        lax.fori_loop(0, tm // SUBLANES, eight, 0)

    def wait_gather():
        pltpu.make_async_copy(xbuf, xbuf, gsem).wait()

    def scatter_copy(r):
        dd = dst_ref[0, 0, r]
        return pltpu.make_async_copy(stage.at[:, r, :], out_hbm.at[dd & 1, lax.shift_right_logical(dd, 1)], ssem)

    def for_valid_rows(tile, fn):
        nv = nvalid_ref[tile]

        def eight(g, c):
            for u in range(SUBLANES):
                fn(g * SUBLANES + u, u)
            return c
        lax.fori_loop(0, nv // SUBLANES, eight, 0)

        def one(r, c):
            fn(r, 0)
            return c
        lax.fori_loop((nv // SUBLANES) * SUBLANES, nv, one, 0)

    def wait_scatter(tile):
        for_valid_rows(tile, lambda r, u: scatter_copy(0).wait())

    @pl.when((i < nused) & (j == 0))
    def _():
        @pl.when(i == 0)
        def _():
            stage[n_chunks:pitch] = jnp.zeros((pitch - n_chunks, tm, LANES), F32)
            start_gather(tok_ref)

        wait_gather()
        a_ref[...] = jnp.concatenate([xbuf[s] for s in range(n_chunks)], axis=1).astype(BF16)

        @pl.when(i + 1 < nused)
        def _():
            start_gather(tokn_ref)

    @pl.when(i < nused)
    def _():
        a = a_ref[...]
        hg = _dot(a, wg_ref[...])
        hu = _dot(a, wu_ref[...])
        hid = (hg * _sigmoid(hg) * hu * gw_ref[...]).astype(BF16)

        @pl.when((j == 0) & (i > 0))
        def _():
            wait_scatter(i - 1)

        cw = 4 * LANES
        for c0 in range(0, wd_ref.shape[1], cw):
            yc = _dot(hid, wd_ref[:, c0:c0 + cw])

            @pl.when(j == 0)
            def _():
                for s in range(cw // LANES):
                    stage[c0 // LANES + s] = yc[:, s * LANES:(s + 1) * LANES]

            @pl.when(j > 0)
            def _():
                for s in range(cw // LANES):
                    stage[c0 // LANES + s] += yc[:, s * LANES:(s + 1) * LANES]

    @pl.when((i < nused) & (j == nf - 1))
    def _():
        for_valid_rows(i, lambda r, u: scatter_copy(r).start(priority=u % 2))

        @pl.when(i == nused - 1)
        def _():
            wait_scatter(i)


def _moe(h_tm, row_tok, row_dst, row_gw, tile_e, nused, tile_nv, wg, wu, wd, tm):
    n_tok, pitch, _ = h_tm.shape
    n_chunks = pitch - SUBLANES
    d = n_chunks * LANES
    n_tiles = row_tok.shape[0] // tm
    f = wg.shape[2]
    tf = _largest_tile(f, 512, LANES)
    kern = functools.partial(_moe_kernel, tm=tm)
    idx3 = row_tok.reshape(n_tiles, 1, tm)
    dst3 = row_dst.reshape(n_tiles, 1, tm)
    smem_blk = lambda im: pl.BlockSpec((1, 1, tm), im, memory_space=pltpu.SMEM)
    grid_spec = pltpu.PrefetchScalarGridSpec(
        num_scalar_prefetch=3,
        grid=(n_tiles, f // tf),
        in_specs=[smem_blk(lambda i, j, te, nu, nv: (i, 0, 0)),
                  smem_blk(lambda i, j, te, nu, nv: (jnp.minimum(i + 1, n_tiles - 1), 0, 0)),
                  smem_blk(lambda i, j, te, nu, nv: (i, 0, 0)),
                  pl.BlockSpec((tm, 1), lambda i, j, te, nu, nv: (i, 0)),
                  pl.BlockSpec(memory_space=pl.ANY),
                  pl.BlockSpec((None, d, tf), lambda i, j, te, nu, nv: (te[i], 0, j)),
                  pl.BlockSpec((None, d, tf), lambda i, j, te, nu, nv: (te[i], 0, j)),
                  pl.BlockSpec((None, tf, d), lambda i, j, te, nu, nv: (te[i], j, 0))],
        out_specs=pl.BlockSpec(memory_space=pl.ANY),
        scratch_shapes=[pltpu.VMEM((n_chunks, tm, LANES), F32), pltpu.VMEM((tm, d), BF16),
                        pltpu.VMEM((pitch, tm, LANES), F32),
                        pltpu.SemaphoreType.DMA(()), pltpu.SemaphoreType.DMA(())],
    )

    def wrapped(te_ref, nu_ref, nv_ref, *rest):
        kern(nu_ref, nv_ref, *rest)

    return pl.pallas_call(
        wrapped,
        grid_spec=grid_spec,
        out_shape=jax.ShapeDtypeStruct((2, n_tok, pitch, LANES), F32),
        compiler_params=pltpu.CompilerParams(dimension_semantics=("arbitrary", "arbitrary"),
                                             vmem_limit_bytes=MOE_VMEM_LIMIT_BYTES),
        name="moe_ffn",
    )(tile_e, nused, tile_nv, idx3, idx3, dst3, row_gw.reshape(-1, 1), h_tm, wg, wu, wd)


def _route_plan(ids, ew, n_experts, tm, n_tiles):
    n = ids.shape[0]
    ea = ids.reshape(-1)
    onehot = (ea[:, None] == jnp.arange(n_experts, dtype=jnp.int32)[None, :]).astype(jnp.int32)
    cs = jnp.cumsum(onehot, axis=0)
    rank = jnp.take_along_axis(cs, ea[:, None], axis=1)[:, 0] - 1
    cnt = cs[-1]
    pcnt = ((cnt + tm - 1) // tm) * tm
    ends = jnp.cumsum(pcnt)
    starts = ends - pcnt
    pos = starts[ea] + rank
    p_rows = n_tiles * tm
    a_idx = jnp.arange(2 * n, dtype=jnp.int32)
    inv = jnp.zeros((p_rows,), jnp.int32).at[pos].set(a_idx + 1)
    row_dst = jnp.maximum(inv - 1, 0)
    row_tok = row_dst // 2
    row_gw = jnp.where(inv > 0, ew.reshape(-1)[row_dst], 0.0)
    nused = (ends[-1] // tm).astype(jnp.int32)
    tile_start = jnp.arange(n_tiles, dtype=jnp.int32) * tm
    tile_e = jnp.sum((tile_start[:, None] >= ends[None, :]).astype(jnp.int32), axis=1)
    tile_e = jnp.minimum(tile_e, n_experts - 1)
    tile_nv = jnp.clip(starts[tile_e] + cnt[tile_e] - tile_start, 0, tm).astype(jnp.int32)
    last_e = tile_e[jnp.maximum(nused - 1, 0)]
    in_use = jnp.arange(n_tiles) < nused
    return (row_tok, row_dst, row_gw, jnp.where(in_use, tile_e, last_e), nused.reshape(1),
            jnp.where(in_use, tile_nv, 0))


def _ln2_kernel(h_ref, m0_ref, m1_ref, g_ref, b_ref, of_ref, ob_ref, *, alpha, pitch):
    tm, d = of_ref.shape
    chunk = lambda ref, s: ref[pl.ds(s, tm, stride=pitch), :]
    x = jnp.concatenate([alpha * chunk(h_ref, s) + (chunk(m0_ref, s) + chunk(m1_ref, s)) for s in range(d // LANES)],
                        axis=1)
    y = _layernorm(x, g_ref[...], b_ref[...])
    of_ref[...] = y
    ob_ref[...] = y.astype(BF16)


def _ln2(h_tm, moe_tm, g, b, alpha):
    d = g.shape[0]
    pitch = _slab_pitch(d)
    n = h_tm.shape[0] // pitch
    tm = _largest_tile(n, 160, 16)
    return pl.pallas_call(
        functools.partial(_ln2_kernel, alpha=alpha, pitch=pitch),
        grid=(n // tm,),
        in_specs=[pl.BlockSpec((tm * pitch, LANES), lambda i: (i, 0)),
                  pl.BlockSpec((None, tm * pitch, LANES), lambda i: (0, i, 0)),
                  pl.BlockSpec((None, tm * pitch, LANES), lambda i: (1, i, 0)),
                  pl.BlockSpec((1, d), lambda i: (0, 0)), pl.BlockSpec((1, d), lambda i: (0, 0))],
        out_specs=[pl.BlockSpec((tm, d), lambda i: (i, 0)), pl.BlockSpec((tm, d), lambda i: (i, 0))],
        out_shape=[jax.ShapeDtypeStruct((n, d), F32), jax.ShapeDtypeStruct((n, d), BF16)],
        compiler_params=_cparams(("parallel",)),
        name="ln2",
    )(h_tm, moe_tm, moe_tm, g.reshape(1, d), b.reshape(1, d))


def kernel(x_prompt, x_sample, cache_k, cache_v, page_table, state_ssm, state_conv, meta_tokens, ln_emb_g, ln_emb_b,
           w_in, conv_w, conv_b, dt_bias, a_log, d_skip, ssd_norm_g, w_ssd_out, w_sb_out, sb_bias, w_o, ln1_g, ln1_b,
           wr_group, br_group, wr_expert, br_expert, w_gate, w_up, w_down, ln2_g, ln2_b):
    n_b, seq, d = x_prompt.shape
    n_db, dseq, _ = x_sample.shape
    depth = w_in.shape[0]
    c_dim = conv_w.shape[-1]
    n_ssd_heads = dt_bias.shape[-1]
    p_dim = d // n_ssd_heads
    nst = (c_dim - d) // (2 * N_SSD_GROUPS)
    sb_heads = sb_bias.shape[-1]
    hd = cache_k.shape[-1]
    sbw = sb_heads * hd
    page = cache_k.shape[2]
    n_exp = w_gate.shape[1]
    per_group = n_exp // N_ROUTE_GROUPS
    alpha = float((2 * depth) ** 0.25)
    n_real = n_b * seq
    n_meta = n_b * N_META
    n_samp = n_db * dseq
    n_tok = n_real + n_meta + n_samp
    assert seq % SSD_CHUNK == 0 and n_ssd_heads <= LANES and N_ROUTE_GROUPS + n_exp <= LANES and d % 512 == 0

    sizes = (d, c_dim, n_ssd_heads, sbw, sbw, sbw, d, d)
    offs = [0]
    for s in sizes:
        offs.append(offs[-1] + s)
    o_z, o_xbc, o_dt, o_q, o_k, o_v, o_gs, o_gb = offs[:8]

    moe_tm = 512 if n_tok >= 4096 else 64
    n_tiles = (2 * n_tok + n_exp * (moe_tm - 1)) // moe_tm + 1

    x_all = jnp.concatenate([x_prompt.reshape(n_real, d),
                             jnp.broadcast_to(meta_tokens[None].astype(x_prompt.dtype), (n_b, N_META, d)).reshape(n_meta, d),
                             x_sample.reshape(n_samp, d)], axis=0)
    h, h_bf = _embed_ln(x_all, ln_emb_g, ln_emb_b)

    lane_pad = lambda v: jnp.pad(v.astype(F32), (0, LANES - v.shape[0])).reshape(1, LANES)
    eexp = (jnp.arange(LANES)[:, None] == (jnp.arange(d)[None, :] // p_dim)).astype(BF16)
    cache_k4 = cache_k.reshape(cache_k.shape[0], depth, page * sb_heads, hd)
    cache_v4 = cache_v.reshape(cache_v.shape[0], depth, page * sb_heads, hd)

    outs = {k: [] for k in ("sp", "cp", "ss", "cs")}
    kv_all = jnp.zeros((depth, n_tok, 2 * sbw), F32)
    for l in range(depth):
        wl = w_in[l]
        seg = lambda c0, width: wl[:, c0:c0 + width].astype(BF16)
        w_dt = jnp.pad(wl[:, o_dt:o_dt + n_ssd_heads], ((0, 0), (0, LANES - n_ssd_heads))).astype(BF16)

        u_z = _matmul(h_bf, seg(o_z, d), BF16, "in_proj_z")
        u_q = _matmul(h_bf, seg(o_q, sbw), BF16, "in_proj_q")
        u_g = _matmul(h_bf, seg(o_gs, 2 * d), BF16, "in_proj_gates")
        u_xbc = _matmul(h_bf, seg(o_xbc, c_dim), F32, "in_proj_xbc")
        u_dt = _matmul(h_bf, w_dt, F32, "in_proj_dt")
        kv_all = _matmul_into_layer(h_bf, seg(o_k, 2 * sbw), kv_all, l, "in_proj_kv")

        consts = (jnp.pad(conv_w[l], ((0, SUBLANES - conv_w.shape[1]), (0, 0))), conv_b[l].reshape(1, c_dim),
                  lane_pad(dt_bias[l]), lane_pad(-jnp.exp(a_log[l].astype(F32))),
                  jnp.repeat(d_skip[l], p_dim).reshape(1, d), ssd_norm_g[l].reshape(1, d), eexp)
        xm = u_xbc[n_real:n_real + n_meta]
        y_meta, st_meta = _ssd_small(xm, jnp.zeros((n_b, SUBLANES, c_dim), F32),
                                     u_z[n_real:n_real + n_meta].astype(F32), u_dt[n_real:n_real + n_meta],
                                     jnp.zeros((n_b, nst, d), F32), consts, n_b, N_META, d, n_ssd_heads, nst, "ssd_meta")
        y_real, st_real = _ssd_main(u_xbc, u_z, u_dt, st_meta, consts, n_b, seq, d, n_ssd_heads, nst)
        xs_ = u_xbc[n_real + n_meta:]
        prev_s = jnp.pad(state_conv[l], ((0, 0), (SUBLANES - state_conv.shape[2], 0), (0, 0)))
        st0_s = jnp.transpose(state_ssm[l].astype(F32), (0, 3, 1, 2)).reshape(n_db, nst, d)
        y_samp, st_samp = _ssd_small(xs_, prev_s, u_z[n_real + n_meta:].astype(F32), u_dt[n_real + n_meta:],
                                     st0_s, consts, n_db, dseq, d, n_ssd_heads, nst, "ssd_sample")
        y_ssd = jnp.concatenate([y_real, y_meta.astype(BF16), y_samp.astype(BF16)], axis=0)

        o_real, o_meta = _sb_prompt(sb_bias[l].astype(F32), u_q, kv_all, l, n_b, seq, sb_heads, hd)
        q_s = u_q[n_real + n_meta:].astype(F32)
        kv_s = kv_all[l, n_real + n_meta:]
        o_samp = _sb_sample(page_table, q_s, kv_s[:, :sbw], kv_s[:, sbw:],
                            jnp.repeat(sb_bias[l].astype(F32), dseq).reshape(1, sb_heads * dseq),
                            cache_k4, cache_v4, l, dseq)
        o_sb = jnp.concatenate([o_real, o_meta, o_samp.astype(BF16)], axis=0)

        merged = _merge(y_ssd, o_sb, w_ssd_out[l].astype(BF16), w_sb_out[l].astype(BF16), u_g, 0, d)
        wr = jnp.concatenate([wr_group[l], jnp.transpose(wr_expert[l], (1, 0, 2)).reshape(d, n_exp)], axis=1)
        wr = jnp.pad(wr.astype(F32), ((0, 0), (0, LANES - wr.shape[1])))
        wr_hi = wr.astype(BF16)
        wr_lo = (wr - wr_hi.astype(F32)).astype(BF16)
        br = lane_pad(jnp.concatenate([br_group[l], br_expert[l].reshape(-1)]))
        h1_tm, ids, ew = _mixln(merged, w_o[l].astype(BF16), h, ln1_g[l], ln1_b[l], wr_hi, wr_lo, br, alpha,
                                N_ROUTE_GROUPS, per_group)
        plan = _route_plan(ids[:, :2], ew[:, :2], n_exp, moe_tm, n_tiles)
        pitch = _slab_pitch(d)
        moe_tm_out = _moe(h1_tm.reshape(n_tok, pitch, LANES), *plan, w_gate[l].astype(BF16), w_up[l].astype(BF16),
                          w_down[l].astype(BF16), moe_tm)
        h, h_bf = _ln2(h1_tm, moe_tm_out.reshape(2, n_tok * pitch, LANES), ln2_g[l], ln2_b[l], alpha)

        to_state = lambda st, nb: jnp.transpose(st.reshape(nb, nst, n_ssd_heads, p_dim), (0, 2, 3, 1))
        outs["sp"].append(to_state(st_real, n_b)); outs["ss"].append(to_state(st_samp, n_db))
        outs["cp"].append(jnp.stack([u_xbc[(b + 1) * seq - 3:(b + 1) * seq] for b in range(n_b)]))
        outs["cs"].append(xs_.reshape(n_db, dseq, c_dim)[:, dseq - 3:, :])

    def kv_out(col0):
        real = kv_all[:, :n_real, col0:col0 + sbw].reshape(depth, n_b, seq, sb_heads, hd)
        meta = kv_all[:, n_real:n_real + n_meta, col0:col0 + sbw].reshape(depth, n_b, N_META, sb_heads, hd)
        samp = kv_all[:, n_real + n_meta:, col0:col0 + sbw].reshape(depth, n_db, dseq, sb_heads, hd)
        return jnp.concatenate([meta, real], axis=2), samp
    kp, ks = kv_out(0)
    vp, vs = kv_out(sbw)
    st = lambda k: jnp.stack(outs[k])
    return (h[:n_real].reshape(n_b, seq, d), h[n_real + n_meta:].reshape(n_db, dseq, d),
            kp, vp, st("sp"), st("cp"), ks, vs, st("ss"), st("cs"))
```
